```python
import math
import jax, jax.numpy as jnp
from jax import lax
import numpy as np

D_MODEL = 2048
BATCH = 2
SEQ = 4096
DEPTH = 4

EXPAND = 2
D_INNER = EXPAND * D_MODEL
N_MIXERS = 2
POOL_WINDOWS = (2, 4, 8, 16)
POOL_GROUPS = len(POOL_WINDOWS)
POOL_GROUP_DIM = D_INNER // POOL_GROUPS
SSM_GROUP_DIM = 16
SSM_GROUPS = D_INNER // SSM_GROUP_DIM
SSM_STATE = 64
SSM_GROUP_CHUNK = 32
SSM_N_CHUNKS = SSM_GROUPS // SSM_GROUP_CHUNK
DT_MIN = 1e-3
DT_MAX = 1e-1
NORM_EPS = 1e-6
N_POOL_LAYERS = (DEPTH + 1) // 2
N_SSM_LAYERS = DEPTH // 2

kernel_name = "hybrid_pool_s5_interleaved"


def _rmsnorm(x, g):
    x32 = x.astype(jnp.float32)
    inv = lax.rsqrt(jnp.mean(x32 * x32, axis=-1, keepdims=True) + NORM_EPS)
    return (x32 * inv * g.astype(jnp.float32)).astype(x.dtype)


def _pool_mixer(u, w_grp, scale):
    b, l, _ = u.shape
    u32 = u.astype(jnp.float32)
    cs = jnp.cumsum(u32, axis=1)
    pos = jnp.arange(l)
    outs = []
    for g, w in enumerate(POOL_WINDOWS):
        sl = slice(g * POOL_GROUP_DIM, (g + 1) * POOL_GROUP_DIM)
        csg = cs[..., sl]
        lag = jnp.pad(csg, ((0, 0), (w, 0), (0, 0)))[:, :l]
        cnt = jnp.minimum(pos + 1, w).astype(jnp.float32)[None, :, None]
        outs.append((csg - lag) / cnt - u32[..., sl])
    p = jnp.stack(outs, axis=2)
    m = jnp.einsum('blgc,gcd->blgd', p, w_grp.astype(jnp.float32)).reshape(b, l, D_INNER)
    return m * scale.astype(jnp.float32)


def _cplx_linrec(e1, e2):
    a1r, a1i, b1r, b1i = e1
    a2r, a2i, b2r, b2i = e2
    ar = a2r * a1r - a2i * a1i
    ai = a2r * a1i + a2i * a1r
    br = a2r * b1r - a2i * b1i + b2r
    bi = a2r * b1i + a2i * b1r + b2i
    return (ar, ai, br, bi)


def _ssm_chunk(args):
    u_c, abr, abi, bbr, bbi, cr, ci = args
    bu_r = jnp.einsum('blgp,gnp->blgn', u_c, bbr)
    bu_i = jnp.einsum('blgp,gnp->blgn', u_c, bbi)
    ar = jnp.broadcast_to(abr, bu_r.shape)
    ai = jnp.broadcast_to(abi, bu_r.shape)
    _, _, hr, hi = lax.associative_scan(_cplx_linrec, (ar, ai, bu_r, bu_i), axis=1)
    return jnp.einsum('blgn,gpn->blgp', hr, cr) - jnp.einsum('blgn,gpn->blgp', hi, ci)


def _s5_mixer(u, a_re, a_im, log_dt, b_re, b_im, c_re, c_im, d_skip, w_glu, b_glu):
    f32 = jnp.float32
    b, l, _ = u.shape
    u32 = u.astype(f32)
    a_re = a_re.astype(f32); a_im = a_im.astype(f32)
    dt = jnp.exp(log_dt.astype(f32))[:, None]
    mag = jnp.exp(a_re * dt)
    abr = mag * jnp.cos(a_im * dt)
    abi = mag * jnp.sin(a_im * dt)
    den = a_re * a_re + a_im * a_im
    nr = abr - 1.0
    fr = (nr * a_re + abi * a_im) / den
    fi = (abi * a_re - nr * a_im) / den
    b_re = b_re.astype(f32); b_im = b_im.astype(f32)
    bbr = fr[..., None] * b_re - fi[..., None] * b_im
    bbi = fr[..., None] * b_im + fi[..., None] * b_re
    nc, gc = SSM_N_CHUNKS, SSM_GROUP_CHUNK
    u_ch = u32.reshape(b, l, nc, gc, SSM_GROUP_DIM).transpose(2, 0, 1, 3, 4)
    y = lax.map(_ssm_chunk, (u_ch,
                             abr.reshape(nc, gc, SSM_STATE), abi.reshape(nc, gc, SSM_STATE),
                             bbr.reshape(nc, gc, SSM_STATE, SSM_GROUP_DIM),
                             bbi.reshape(nc, gc, SSM_STATE, SSM_GROUP_DIM),
                             c_re.astype(f32).reshape(nc, gc, SSM_GROUP_DIM, SSM_STATE),
                             c_im.astype(f32).reshape(nc, gc, SSM_GROUP_DIM, SSM_STATE)))
    y = y.transpose(1, 2, 0, 3, 4).reshape(b, l, D_INNER) + d_skip.astype(f32) * u32
    g = jax.nn.gelu(y)
    return g * jax.nn.sigmoid(g @ w_glu.astype(f32) + b_glu.astype(f32))


def setup_inputs(seed: int = 0) -> dict:
    key = jax.random.key(seed)
    ks = jax.random.split(key, 24)
    f32 = jnp.float32
    D, E, G, N, P = D_MODEL, D_INNER, SSM_GROUPS, SSM_STATE, SSM_GROUP_DIM
    NP_, NS_ = N_POOL_LAYERS, N_SSM_LAYERS
    nrm = lambda k, s, sc: jax.random.normal(k, s, f32) * sc
    x = jax.random.normal(ks[0], (BATCH, SEQ, D), f32)
    norm_g = 1.0 + nrm(ks[1], (DEPTH, D), 0.02)
    final_norm_g = 1.0 + nrm(ks[2], (D,), 0.02)
    pool_w_in = nrm(ks[3], (NP_, D, 2 * E), D ** -0.5)
    pool_w_grp = nrm(ks[4], (NP_, POOL_GROUPS, POOL_GROUP_DIM, POOL_GROUP_DIM), POOL_GROUP_DIM ** -0.5)
    pool_scale = 1.0 + nrm(ks[5], (NP_, E), 0.02)
    pool_w_out = nrm(ks[6], (NP_, E, D), E ** -0.5)
    ssm_w_in = nrm(ks[7], (NS_, D, 2 * E), D ** -0.5)
    n_idx = jnp.arange(N, dtype=f32)
    ssm_a_re = -0.5 + nrm(ks[8], (NS_, G, N), 0.01)
    ssm_a_im = math.pi * n_idx[None, None, :] + nrm(ks[9], (NS_, G, N), 0.01)
    ssm_log_dt = math.log(DT_MIN) + jax.random.uniform(ks[10], (NS_, G), f32) * (math.log(DT_MAX) - math.log(DT_MIN))
    ssm_b_re = nrm(ks[11], (NS_, G, N, P), (2.0 * P) ** -0.5)
    ssm_b_im = nrm(ks[12], (NS_, G, N, P), (2.0 * P) ** -0.5)
    ssm_c_re = nrm(ks[13], (NS_, G, P, N), (2.0 * N) ** -0.5)
    ssm_c_im = nrm(ks[14], (NS_, G, P, N), (2.0 * N) ** -0.5)
    ssm_d = nrm(ks[15], (NS_, E), 1.0)
    ssm_w_glu = nrm(ks[16], (NS_, E, E), E ** -0.5)
    ssm_b_glu = nrm(ks[17], (NS_, E), 0.02)
    ssm_w_out = nrm(ks[18], (NS_, E, D), E ** -0.5)
    return {"x": x, "norm_g": norm_g, "final_norm_g": final_norm_g,
            "pool_w_in": pool_w_in, "pool_w_grp": pool_w_grp, "pool_scale": pool_scale, "pool_w_out": pool_w_out,
            "ssm_w_in": ssm_w_in, "ssm_a_re": ssm_a_re, "ssm_a_im": ssm_a_im, "ssm_log_dt": ssm_log_dt,
            "ssm_b_re": ssm_b_re, "ssm_b_im": ssm_b_im, "ssm_c_re": ssm_c_re, "ssm_c_im": ssm_c_im,
            "ssm_d": ssm_d, "ssm_w_glu": ssm_w_glu, "ssm_b_glu": ssm_b_glu, "ssm_w_out": ssm_w_out}


def reference(x, norm_g, final_norm_g, pool_w_in, pool_w_grp, pool_scale, pool_w_out,
              ssm_w_in, ssm_a_re, ssm_a_im, ssm_log_dt, ssm_b_re, ssm_b_im, ssm_c_re, ssm_c_im,
              ssm_d, ssm_w_glu, ssm_b_glu, ssm_w_out):
    for i in range(DEPTH):
        h = _rmsnorm(x, norm_g[i])
        j = i // N_MIXERS
        if i % N_MIXERS == 0:
            uz = h @ pool_w_in[j]
            u, z = uz[..., :D_INNER], uz[..., D_INNER:]
            y = _pool_mixer(u, pool_w_grp[j], pool_scale[j])
            w_out = pool_w_out[j]
        else:
            uz = h @ ssm_w_in[j]
            u, z = uz[..., :D_INNER], uz[..., D_INNER:]
            y = _s5_mixer(u, ssm_a_re[j], ssm_a_im[j], ssm_log_dt[j], ssm_b_re[j], ssm_b_im[j],
                          ssm_c_re[j], ssm_c_im[j], ssm_d[j], ssm_w_glu[j], ssm_b_glu[j])
            w_out = ssm_w_out[j]
        gated = (y * jax.nn.silu(z.astype(jnp.float32))).astype(x.dtype)
        x = x + (gated @ w_out).astype(x.dtype)
    return _rmsnorm(x, final_norm_g)
```

```python
import functools
import math

import jax
import jax.numpy as jnp
from jax import lax
from jax.experimental import pallas as pl
from jax.experimental.pallas import tpu as pltpu

F32 = jnp.float32
BF16 = jnp.bfloat16

POOL_WINDOWS = (2, 4, 8, 16)
HALO = 16
SSM_P = 16
SSM_N = 64
CHUNK = 16
SEGS = 8
NORM_EPS = 1e-6
VMEM_LIMIT = 56 * 1024 * 1024


def _cparams():
    return pltpu.CompilerParams(vmem_limit_bytes=VMEM_LIMIT)


def _rmsnorm_kernel(x_ref, g_ref, o_ref):
    x = x_ref[...]
    inv = lax.rsqrt(jnp.mean(x * x, axis=-1, keepdims=True) + NORM_EPS)
    o_ref[...] = (x * inv * g_ref[...]).astype(o_ref.dtype)


def _rmsnorm(x, g, tm=512):
    t, d = x.shape
    return pl.pallas_call(
        _rmsnorm_kernel,
        grid=(t // tm,),
        in_specs=[pl.BlockSpec((tm, d), lambda i: (i, 0)),
                  pl.BlockSpec((1, d), lambda i: (0, 0))],
        out_specs=pl.BlockSpec((tm, d), lambda i: (i, 0)),
        out_shape=jax.ShapeDtypeStruct((t, d), BF16),
        compiler_params=_cparams(),
        name="rmsnorm",
    )(x, g.reshape(1, d))


def _matmul_kernel(a_ref, b_ref, o_ref):
    o_ref[...] = jnp.dot(a_ref[...], b_ref[...],
                         preferred_element_type=F32).astype(o_ref.dtype)


def _in_proj(h, w, tm=1024, tn=1024):
    t, k = h.shape
    n = w.shape[1]
    return pl.pallas_call(
        _matmul_kernel,
        grid=(n // tn, t // tm),
        in_specs=[pl.BlockSpec((tm, k), lambda j, i: (i, 0)),
                  pl.BlockSpec((k, tn), lambda j, i: (0, j))],
        out_specs=pl.BlockSpec((tm, tn), lambda j, i: (i, j)),
        out_shape=jax.ShapeDtypeStruct((t, n), BF16),
        compiler_params=_cparams(),
        name="in_proj",
    )(h, w)


def _pool_kernel(u_ref, halo_ref, z_ref, wg_ref, sc_ref, o_ref, *, tm, seq_len, cg):
    i = pl.program_id(0)
    tile_in_seq = i % (seq_len // tm)
    first = tile_in_seq == 0
    pos = lax.broadcasted_iota(jnp.int32, (tm, 1), 0) + tile_in_seq * tm
    for g, w in enumerate(POOL_WINDOWS):
        cols = slice(g * cg, (g + 1) * cg)
        halo = halo_ref[:, cols].astype(F32)
        halo = jnp.where(first, 0.0, halo)
        cur = u_ref[:, cols].astype(F32)
        s = jnp.concatenate([halo, cur], axis=0)
        k = 1
        while k < w:
            s = s + pltpu.roll(s, k, axis=0)
            k *= 2
        cnt = jnp.minimum(pos + 1, w).astype(F32)
        p = s[HALO:] * (1.0 / cnt) - cur
        m = jnp.dot(p.astype(BF16), wg_ref[g], preferred_element_type=F32)
        zz = z_ref[:, cols].astype(F32)
        y = m * sc_ref[:, cols] * (zz * jax.nn.sigmoid(zz))
        o_ref[:, cols] = y.astype(o_ref.dtype)


def _pool_mixer(uz, w_grp, scale, seq_len, tm=256):
    t, e2 = uz.shape
    e = e2 // 2
    ng, cg, _ = w_grp.shape
    hb = tm // HALO
    kern = functools.partial(_pool_kernel, tm=tm, seq_len=seq_len, cg=cg)
    return pl.pallas_call(
        kern,
        grid=(t // tm,),
        in_specs=[pl.BlockSpec((tm, e), lambda i: (i, 0)),
                  pl.BlockSpec((HALO, e), lambda i: (jnp.maximum(i * hb - 1, 0), 0)),
                  pl.BlockSpec((tm, e), lambda i: (i, 1)),
                  pl.BlockSpec((ng, cg, cg), lambda i: (0, 0, 0)),
                  pl.BlockSpec((1, e), lambda i: (0, 0))],
        out_specs=pl.BlockSpec((tm, e), lambda i: (i, 0)),
        out_shape=jax.ShapeDtypeStruct((t, e), BF16),
        compiler_params=_cparams(),
        name="pool_mixer",
    )(uz, uz, uz, w_grp, scale.reshape(1, e))


def _cmul(c1, c2, p, q):
    return c1 * p + c2 * q, c1 * q - c2 * p


def _ssm_kernel(u_ref, k_ref, wb_ref, wc_ref, co_ref, d_ref, o_ref, s_ref, h_ref,
                *, gb, gs, nb, rows):
    steps = rows // SEGS
    half = 2 * SSM_N

    for j in range(gb):
        s = jnp.dot(u_ref[j], wb_ref[j], preferred_element_type=F32)
        s_ref[j] = s.reshape(nb, rows, 2 * half)

    for j0 in range(0, gb, gs):
        js = slice(j0, j0 + gs)
        co = co_ref[js]
        c1 = co[:, 0:1, :][:, None]
        c2 = co[:, 1:2, :][:, None]
        shape = (gs, nb, SEGS, half)
        c1 = jnp.broadcast_to(c1, (gs, 1, SEGS, half))
        c2 = jnp.broadcast_to(c2, (gs, 1, SEGS, half))

        def local_scan(i, carry):
            p, q = carry
            r0 = pl.multiple_of(i * SEGS, SEGS)
            tile = s_ref[js, :, pl.ds(r0, SEGS), :]
            ap, aq = _cmul(c1, c2, p, q)
            return ap + tile[..., :half], aq + tile[..., half:]

        zero = jnp.zeros(shape, F32)
        ep, eq = lax.fori_loop(0, steps, local_scan, (zero, zero))

        seg = lax.broadcasted_iota(jnp.int32, shape, 2)
        xp, xq = ep, eq
        for lvl, k in enumerate((1, 2, 4)):
            d1 = co[:, 2 + 2 * lvl:3 + 2 * lvl, :][:, None]
            d2 = co[:, 3 + 2 * lvl:4 + 2 * lvl, :][:, None]
            rp = pltpu.roll(xp.reshape(gs * nb, SEGS, half), k, axis=1).reshape(shape)
            rq = pltpu.roll(xq.reshape(gs * nb, SEGS, half), k, axis=1).reshape(shape)
            mp, mq = _cmul(d1, d2, rp, rq)
            keep = seg >= k
            xp = xp + jnp.where(keep, mp, 0.0)
            xq = xq + jnp.where(keep, mq, 0.0)
        gp = jnp.where(seg >= 1,
                       pltpu.roll(xp.reshape(gs * nb, SEGS, half), 1, axis=1).reshape(shape), 0.0)
        gq = jnp.where(seg >= 1,
                       pltpu.roll(xq.reshape(gs * nb, SEGS, half), 1, axis=1).reshape(shape), 0.0)

        def global_scan(i, carry):
            p, q = carry
            r0 = pl.multiple_of(i * SEGS, SEGS)
            h_ref[js, :, pl.ds(r0, SEGS), :] = p
            tile = s_ref[js, :, pl.ds(r0, SEGS), :]
            ap, aq = _cmul(c1, c2, p, q)
            return ap + tile[..., :half], aq + tile[..., half:]

        lax.fori_loop(0, steps, global_scan, (gp, gq))

    for j in range(gb):
        u = u_ref[j]
        hin = h_ref[j].reshape(nb * rows, half).astype(BF16)
        y = jnp.dot(u, k_ref[j], preferred_element_type=F32)
        y = y + jnp.dot(hin, wc_ref[j], preferred_element_type=F32)
        y = y + d_ref[j] * u.astype(F32)
        o_ref[j] = jax.nn.gelu(y).astype(o_ref.dtype)


def _ssm_mixer(u_c, kt, wb, wc, coef, d_c, nb, gb=8, gs=4):
    g, m, kk = u_c.shape
    rows = m // nb
    half = 2 * SSM_N
    kern = functools.partial(_ssm_kernel, gb=gb, gs=gs, nb=nb, rows=rows)
    return pl.pallas_call(
        kern,
        grid=(g // gb,),
        in_specs=[pl.BlockSpec((gb, m, kk), lambda i: (i, 0, 0)),
                  pl.BlockSpec((gb, kk, kk), lambda i: (i, 0, 0)),
                  pl.BlockSpec((gb, kk, 2 * half), lambda i: (i, 0, 0)),
                  pl.BlockSpec((gb, half, kk), lambda i: (i, 0, 0)),
                  pl.BlockSpec((gb, 8, half), lambda i: (i, 0, 0)),
                  pl.BlockSpec((gb, 1, kk), lambda i: (i, 0, 0))],
        out_specs=pl.BlockSpec((gb, m, kk), lambda i: (i, 0, 0)),
        out_shape=jax.ShapeDtypeStruct((g, m, kk), BF16),
        scratch_shapes=[pltpu.VMEM((gb, nb, rows, 2 * half), F32),
                        pltpu.VMEM((gb, nb, rows, half), F32)],
        compiler_params=_cparams(),
        name="ssm_mixer",
    )(u_c, kt, wb, wc, coef, d_c)


def _ssm_weights(a_re, a_im, log_dt, b_re, b_im, c_re, c_im, d_skip, seq_len):
    g, n = a_re.shape
    p = b_re.shape[-1]
    dt = jnp.exp(log_dt)[:, None]
    mag = jnp.exp(a_re * dt)
    abr = mag * jnp.cos(a_im * dt)
    abi = mag * jnp.sin(a_im * dt)
    den = a_re * a_re + a_im * a_im
    nr = abr - 1.0
    fr = (nr * a_re + abi * a_im) / den
    fi = (abi * a_re - nr * a_im) / den
    bbr = fr[..., None] * b_re - fi[..., None] * b_im
    bbi = fr[..., None] * b_im + fi[..., None] * b_re

    prs, pis = [jnp.ones_like(abr)], [jnp.zeros_like(abr)]
    for _ in range(CHUNK):
        r, i = prs[-1], pis[-1]
        prs.append(r * abr - i * abi)
        pis.append(r * abi + i * abr)
    pr = jnp.stack(prs, axis=1)
    pi = jnp.stack(pis, axis=1)

    car = c_re[:, None] * pr[:, :, None, :] - c_im[:, None] * pi[:, :, None, :]
    cai = c_re[:, None] * pi[:, :, None, :] + c_im[:, None] * pr[:, :, None, :]
    hi = lax.Precision.HIGHEST
    kd = (jnp.einsum('gdqn,gnp->gdpq', car[:, :CHUNK], bbr, precision=hi)
          - jnp.einsum('gdqn,gnp->gdpq', cai[:, :CHUNK], bbi, precision=hi))
    tau = jnp.arange(CHUNK)
    lag = tau[None, :] - tau[:, None]
    causal = (lag >= 0).astype(F32)
    kt = kd[:, jnp.clip(lag, 0, CHUNK - 1)]
    kt = kt * causal[None, :, :, None, None]
    kt = kt.transpose(0, 1, 3, 2, 4).reshape(g, CHUNK * p, CHUNK * p)

    rr = pr[:, CHUNK - 1::-1][:, :CHUNK]
    ri = pi[:, CHUNK - 1::-1][:, :CHUNK]
    bt_r = bbr.transpose(0, 2, 1)[:, None]
    bt_i = bbi.transpose(0, 2, 1)[:, None]
    wbr = rr[:, :, None, :] * bt_r - ri[:, :, None, :] * bt_i
    wbi = rr[:, :, None, :] * bt_i + ri[:, :, None, :] * bt_r
    wb = jnp.concatenate([wbr, wbi, wbi, wbr], axis=-1).reshape(g, CHUNK * p, 4 * n)

    wcr = car[:, 1:].transpose(0, 3, 1, 2).reshape(g, n, CHUNK * p)
    wci = cai[:, 1:].transpose(0, 3, 1, 2).reshape(g, n, CHUNK * p)
    wc = jnp.concatenate([wcr, -wci], axis=1)

    def pack(r, i):
        return jnp.concatenate([r, r], axis=-1), jnp.concatenate([-i, i], axis=-1)

    r, i = pr[:, CHUNK], pi[:, CHUNK]
    coefs = list(pack(r, i))
    seg_tokens = seq_len // SEGS
    reach = CHUNK
    while reach < seg_tokens:
        r, i = r * r - i * i, 2.0 * r * i
        reach *= 2
    for _ in range(3):
        coefs.extend(pack(r, i))
        r, i = r * r - i * i, 2.0 * r * i
    coef = jnp.stack(coefs, axis=1)

    d_c = jnp.tile(d_skip.reshape(g, 1, p), (1, CHUNK, 1)).reshape(g, 1, CHUNK * p)
    return kt.astype(BF16), wb.astype(BF16), wc.astype(BF16), coef, d_c


def _to_chunks(u, nb, seq_len):
    e = u.shape[1]
    g = e // SSM_P
    per_seg = seq_len // (SEGS * CHUNK)
    u6 = u.reshape(nb, SEGS, per_seg, CHUNK, g, SSM_P)
    return u6.transpose(4, 0, 2, 1, 3, 5).reshape(g, nb * per_seg * SEGS, CHUNK * SSM_P)


def _from_chunks(y_c, nb, seq_len):
    g = y_c.shape[0]
    per_seg = seq_len // (SEGS * CHUNK)
    y6 = y_c.reshape(g, nb, per_seg, SEGS, CHUNK, SSM_P)
    return y6.transpose(1, 3, 2, 4, 0, 5).reshape(nb * seq_len, g * SSM_P)


def _glu_kernel(a_ref, w_ref, b_ref, g_ref, z_ref, o_ref):
    lin = jnp.dot(a_ref[...], w_ref[...], preferred_element_type=F32) + b_ref[...]
    gg = g_ref[...].astype(F32)
    zz = z_ref[...].astype(F32)
    o_ref[...] = (gg * jax.nn.sigmoid(lin) * (zz * jax.nn.sigmoid(zz))).astype(o_ref.dtype)


def _glu(gact, w, b, uz, tm=512, tn=1024):
    t, e = gact.shape
    zoff = e // tn
    return pl.pallas_call(
        _glu_kernel,
        grid=(e // tn, t // tm),
        in_specs=[pl.BlockSpec((tm, e), lambda j, i: (i, 0)),
                  pl.BlockSpec((e, tn), lambda j, i: (0, j)),
                  pl.BlockSpec((1, tn), lambda j, i: (0, j)),
                  pl.BlockSpec((tm, tn), lambda j, i: (i, j)),
                  pl.BlockSpec((tm, tn), lambda j, i: (i, j + zoff))],
        out_specs=pl.BlockSpec((tm, tn), lambda j, i: (i, j)),
        out_shape=jax.ShapeDtypeStruct((t, e), BF16),
        compiler_params=_cparams(),
        name="glu",
    )(gact, w, b.reshape(1, e), gact, uz)


def _out_proj_kernel(y_ref, w_ref, x_ref, g_ref, *o_refs, last):
    xn = x_ref[...] + jnp.dot(y_ref[...], w_ref[...], preferred_element_type=F32)
    inv = lax.rsqrt(jnp.mean(xn * xn, axis=-1, keepdims=True) + NORM_EPS)
    hn = xn * inv * g_ref[...]
    if last:
        o_refs[0][...] = hn
    else:
        o_refs[0][...] = xn
        o_refs[1][...] = hn.astype(o_refs[1].dtype)


def _out_proj(y, w, x, gain, last, tm=256):
    t, e = y.shape
    d = w.shape[1]
    row = pl.BlockSpec((tm, d), lambda i: (i, 0))
    if last:
        out_shape = jax.ShapeDtypeStruct((t, d), F32)
        out_specs = row
    else:
        out_shape = (jax.ShapeDtypeStruct((t, d), F32), jax.ShapeDtypeStruct((t, d), BF16))
        out_specs = (row, row)
    return pl.pallas_call(
        functools.partial(_out_proj_kernel, last=last),
        grid=(t // tm,),
        in_specs=[pl.BlockSpec((tm, e), lambda i: (i, 0)),
                  pl.BlockSpec((e, d), lambda i: (0, 0)),
                  row,
                  pl.BlockSpec((1, d), lambda i: (0, 0))],
        out_specs=out_specs,
        out_shape=out_shape,
        compiler_params=_cparams(),
        name="out_proj",
    )(y, w, x, gain.reshape(1, d))


def kernel(x, norm_g, final_norm_g, pool_w_in, pool_w_grp, pool_scale, pool_w_out, ssm_w_in, ssm_a_re, ssm_a_im, ssm_log_dt, ssm_b_re, ssm_b_im, ssm_c_re, ssm_c_im, ssm_d, ssm_w_glu, ssm_b_glu, ssm_w_out):
    nb, seq_len, d = x.shape
    depth = norm_g.shape[0]
    xs = x.reshape(nb * seq_len, d)
    h = _rmsnorm(xs, norm_g[0])
    for layer in range(depth):
        j = layer // 2
        if layer % 2 == 0:
            uz = _in_proj(h, pool_w_in[j].astype(BF16))
            y = _pool_mixer(uz, pool_w_grp[j].astype(BF16), pool_scale[j], seq_len)
            w_out = pool_w_out[j]
        else:
            uz = _in_proj(h, ssm_w_in[j].astype(BF16))
            e = uz.shape[1] // 2
            kt, wb, wc, coef, d_c = _ssm_weights(
                ssm_a_re[j], ssm_a_im[j], ssm_log_dt[j], ssm_b_re[j], ssm_b_im[j],
                ssm_c_re[j], ssm_c_im[j], ssm_d[j], seq_len)
            u_c = _to_chunks(uz[:, :e], nb, seq_len)
            g_c = _ssm_mixer(u_c, kt, wb, wc, coef, d_c, nb)
            gact = _from_chunks(g_c, nb, seq_len)
            y = _glu(gact, ssm_w_glu[j].astype(BF16), ssm_b_glu[j], uz)
            w_out = ssm_w_out[j]
        last = layer == depth - 1
        gain = final_norm_g if last else norm_g[layer + 1]
        res = _out_proj(y, w_out.astype(BF16), xs, gain, last)
        if last:
            return res.reshape(nb, seq_len, d)
        xs, h = res
```

```python
import functools

import jax
import jax.numpy as jnp
from jax import lax
from jax.experimental import pallas as pl
from jax.experimental.pallas import tpu as pltpu

F32 = jnp.float32
BF16 = jnp.bfloat16

POOL_WINDOWS = (2, 4, 8, 16)
SSM_P = 16
SSM_N = 64
CHUNK = 16
LANES = 128
NORM_EPS = 1e-6
VMEM_LIMIT = 56 * 1024 * 1024

_T_F1, _T_F2, _T_G1, _T_G2, _T_D1, _T_D2, _T_BP, _T_BQ, _T_CR, _T_CI, _T_ROWS = (
    0, 16, 32, 48, 64, 88, 112, 128, 144, 160, 176)


def _cparams():
    return pltpu.CompilerParams(vmem_limit_bytes=VMEM_LIMIT)


def _silu(z):
    return z * jax.nn.sigmoid(z)


def _rmsnorm_kernel(x_ref, g_ref, o_ref):
    x = x_ref[...]
    inv = lax.rsqrt(jnp.mean(x * x, axis=-1, keepdims=True) + NORM_EPS)
    o_ref[...] = (x * inv * g_ref[...]).astype(o_ref.dtype)


def _rmsnorm(x, gains, layer, tm=512):
    t, d = x.shape
    return pl.pallas_call(
        _rmsnorm_kernel,
        grid=(t // tm,),
        in_specs=[pl.BlockSpec((tm, d), lambda i: (i, 0)),
                  pl.BlockSpec((None, 1, d), lambda i: (layer, 0, 0))],
        out_specs=pl.BlockSpec((tm, d), lambda i: (i, 0)),
        out_shape=jax.ShapeDtypeStruct((t, d), BF16),
        compiler_params=_cparams(),
        name="rmsnorm",
    )(x, gains)


def _in_proj_kernel(a_ref, w_ref, o_ref, wb_ref):
    @pl.when(pl.program_id(1) == 0)
    def _():
        wb_ref[...] = w_ref[...].astype(BF16)

    o_ref[...] = jnp.dot(a_ref[...], wb_ref[...],
                         preferred_element_type=F32).astype(o_ref.dtype)


def _in_proj(h, w_all, j, tm=1024, tn=1024):
    t, k = h.shape
    n = w_all.shape[2]
    return pl.pallas_call(
        _in_proj_kernel,
        grid=(n // tn, t // tm),
        in_specs=[pl.BlockSpec((tm, k), lambda c, r: (r, 0)),
                  pl.BlockSpec((None, k, tn), lambda c, r: (j, 0, c))],
        out_specs=pl.BlockSpec((tm, tn), lambda c, r: (r, c)),
        out_shape=jax.ShapeDtypeStruct((t, n), BF16),
        scratch_shapes=[pltpu.VMEM((k, tn), BF16)],
        compiler_params=_cparams(),
        name="in_proj",
    )(h, w_all)


def _pool_kernel(u_ref, prev_ref, z_ref, wg_ref, sc_ref, o_ref, wb_ref, cur_ref, hist_ref, p_ref,
                 *, rows, cg, sub):
    g = pl.program_id(0)
    i = pl.program_id(1)
    b = pl.program_id(2)

    @pl.when((i == 0) & (b == 0))
    def _():
        wb_ref[...] = wg_ref[...].astype(BF16)

    seg0 = lax.broadcasted_iota(jnp.int32, (rows, 1), 0) == 0
    for m in range(1, CHUNK):
        blk = prev_ref[CHUNK - m].astype(F32)
        moved = jnp.where(seg0, 0.0, pltpu.roll(blk, 1, axis=0))
        hist_ref[m - 1] = jnp.where(i == 0, moved, blk)
    for t in range(CHUNK):
        cur_ref[t] = u_ref[t].astype(F32)

    n_r, n_c = rows // 8, cg // LANES
    for gi, w in enumerate(POOL_WINDOWS):
        @pl.when(g == gi)
        def _(w=w):
            def strip(idx, carry):
                r0 = pl.multiple_of((idx // n_c) * 8, 8)
                c0 = pl.multiple_of((idx % n_c) * LANES, LANES)
                rs, cs = pl.ds(r0, 8), pl.ds(c0, LANES)
                cur = [cur_ref[t, rs, cs] for t in range(CHUNK)]
                s = {t: cur[t] for t in range(CHUNK)}
                for m in range(1, w):
                    s[-m] = hist_ref[m - 1, rs, cs]
                lo, k = -(w - 1), 1
                while k < w:
                    s = {t: s[t] + s[t - k] for t in range(lo + k, CHUNK)}
                    lo += k
                    k *= 2
                first = (i == 0) & (lax.broadcasted_iota(jnp.int32, (8, 1), 0) + r0 == 0)
                for t in range(CHUNK):
                    inv = jnp.where(first, 1.0 / min(t + 1, w), 1.0 / w)
                    p_ref[t, rs, cs] = s[t] * inv - cur[t]
                return carry

            lax.fori_loop(0, n_r * n_c, strip, 0)

    for q in range(CHUNK // sub):
        ts = slice(q * sub, (q + 1) * sub)
        p = p_ref[ts].reshape(sub * rows, cg).astype(BF16)
        m = jnp.dot(p, wb_ref[...], preferred_element_type=F32)
        zz = z_ref[ts].reshape(sub * rows, cg).astype(F32)
        y = m * sc_ref[...] * _silu(zz)
        o_ref[ts] = y.astype(o_ref.dtype).reshape(sub, rows, cg)


def _pool_mixer(uz, w_grp_all, scale_all, j, nb, ni):
    t, e2 = uz.shape
    e = e2 // 2
    ng, cg = w_grp_all.shape[1], w_grp_all.shape[2]
    nc = t // CHUNK
    rows = nc // (ni * nb)
    uz3 = uz.reshape(CHUNK, nc, e2)
    blk = (CHUNK, rows, cg)
    kern = functools.partial(_pool_kernel, rows=rows, cg=cg, sub=4)
    out = pl.pallas_call(
        kern,
        grid=(ng, ni, nb),
        in_specs=[pl.BlockSpec(blk, lambda g, i, b: (0, i * nb + b, g)),
                  pl.BlockSpec(blk, lambda g, i, b: (0, ((i + ni - 1) % ni) * nb + b, g)),
                  pl.BlockSpec(blk, lambda g, i, b: (0, i * nb + b, ng + g)),
                  pl.BlockSpec((None, None, cg, cg), lambda g, i, b: (j, g, 0, 0)),
                  pl.BlockSpec((None, 1, cg), lambda g, i, b: (j, 0, g))],
        out_specs=pl.BlockSpec(blk, lambda g, i, b: (0, i * nb + b, g)),
        out_shape=jax.ShapeDtypeStruct((CHUNK, nc, e), BF16),
        scratch_shapes=[pltpu.VMEM((cg, cg), BF16),
                        pltpu.VMEM((CHUNK, rows, cg), F32),
                        pltpu.VMEM((CHUNK - 1, rows, cg), F32),
                        pltpu.VMEM((CHUNK, rows, cg), F32)],
        compiler_params=_cparams(),
        name="pool_mixer",
    )(uz3, uz3, uz3, w_grp_all, scale_all)
    return out.reshape(t, e)


def _gen_group_weights(tab):
    def row(r):
        return jnp.broadcast_to(tab[r:r + 1, :], (SSM_P, 2 * SSM_N))

    bp, bq = tab[_T_BP:_T_BP + SSM_P], tab[_T_BQ:_T_BQ + SSM_P]
    cr, ci = tab[_T_CR:_T_CR + SSM_P], tab[_T_CI:_T_CI + SSM_P]
    wb = jnp.concatenate([bp * row(_T_F1 + t) + bq * row(_T_F2 + t) for t in range(CHUNK)], axis=0)
    rn = jnp.concatenate([bp * row(_T_G1 + t) + bq * row(_T_G2 + t) for t in range(CHUNK)], axis=0)
    cb = [cr * row(_T_D1 + d) + ci * row(_T_D2 + d) for d in range(CHUNK + 1)]
    lm = jnp.concatenate(cb[:CHUNK], axis=0)
    wc = jnp.concatenate(cb[1:], axis=0)
    kt = lax.dot_general(lm, rn, (((1,), (1,)), ((), ())),
                         precision=lax.Precision.HIGHEST, preferred_element_type=F32)
    n = CHUNK * SSM_P
    r_tau = lax.broadcasted_iota(jnp.int32, (n, n), 0) // SSM_P
    c_tau = lax.broadcasted_iota(jnp.int32, (n, n), 1) // SSM_P
    kt = jnp.where(r_tau >= c_tau, kt, 0.0)
    return wb.T.astype(BF16), wc.astype(BF16), kt.astype(BF16)


def _ssm_kernel(u_ref, tab_ref, co_ref, d_ref, o_ref, xt_ref, ot_ref, *, gb, nc, ni, segs):
    n = SSM_N
    kk = CHUNK * SSM_P

    for t in range(CHUNK):
        xt_ref[t] = u_ref[t].astype(F32).T.astype(BF16)

    lane_seg = lax.broadcasted_iota(jnp.int32, (n, LANES), 1) % segs

    def cmul(ar, ai, xr, xi):
        return ar * xr - ai * xi, ar * xi + ai * xr

    def group(j, carry):
        rows = pl.ds(pl.multiple_of(j * SSM_P, SSM_P), SSM_P)
        ut = xt_ref[:, rows, :].reshape(kk, nc)
        wbt, wct, ktt = _gen_group_weights(tab_ref[j])
        st = jnp.dot(wbt, ut, preferred_element_type=F32)
        co = co_ref[j]

        def coef(c):
            return (jnp.broadcast_to(co[:, 2 * c:2 * c + 1], (n, LANES)),
                    jnp.broadcast_to(co[:, 2 * c + 1:2 * c + 2], (n, LANES)))

        ar, ai = coef(0)
        s_r = [st[:n, i * LANES:(i + 1) * LANES] for i in range(ni)]
        s_i = [st[n:, i * LANES:(i + 1) * LANES] for i in range(ni)]

        er, ei = s_r[0], s_i[0]
        for i in range(1, ni):
            mr, mi = cmul(ar, ai, er, ei)
            er, ei = mr + s_r[i], mi + s_i[i]

        lvl, k = 1, 1
        while k < segs:
            mr, mi = coef(lvl)
            rr, ri = cmul(mr, mi, pltpu.roll(er, k, axis=1), pltpu.roll(ei, k, axis=1))
            keep = lane_seg >= k
            er = er + jnp.where(keep, rr, 0.0)
            ei = ei + jnp.where(keep, ri, 0.0)
            lvl += 1
            k *= 2
        hr = jnp.where(lane_seg >= 1, pltpu.roll(er, 1, axis=1), 0.0)
        hi = jnp.where(lane_seg >= 1, pltpu.roll(ei, 1, axis=1), 0.0)

        h_tiles = []
        for i in range(ni):
            h_tiles.append(jnp.concatenate([hr, hi], axis=0).astype(BF16))
            if i + 1 < ni:
                mr, mi = cmul(ar, ai, hr, hi)
                hr, hi = mr + s_r[i], mi + s_i[i]
        ht = jnp.concatenate(h_tiles, axis=1)

        yt = jnp.dot(ktt, ut, preferred_element_type=F32)
        yt = yt + jnp.dot(wct, ht, preferred_element_type=F32)
        ot_ref[:, rows, :] = yt.reshape(CHUNK, SSM_P, nc)
        return carry

    lax.fori_loop(0, gb, group, 0)

    for t in range(CHUNK):
        y = ot_ref[t].T + d_ref[...] * u_ref[t].astype(F32)
        o_ref[t] = jax.nn.gelu(y).astype(o_ref.dtype)


def _ssm_mixer(uz, tab, coef, d_all, j, nb, ni, gb=8):
    t, e2 = uz.shape
    e = e2 // 2
    g = e // SSM_P
    nc = t // CHUNK
    segs = LANES // nb
    uz3 = uz.reshape(CHUNK, nc, e2)
    wl = gb * SSM_P
    kern = functools.partial(_ssm_kernel, gb=gb, nc=nc, ni=ni, segs=segs)
    out = pl.pallas_call(
        kern,
        grid=(g // gb,),
        in_specs=[pl.BlockSpec((CHUNK, nc, wl), lambda i: (0, 0, i)),
                  pl.BlockSpec((gb, _T_ROWS, 2 * SSM_N), lambda i: (i, 0, 0)),
                  pl.BlockSpec((gb, SSM_N, 16), lambda i: (i, 0, 0)),
                  pl.BlockSpec((None, 1, wl), lambda i: (j, 0, i))],
        out_specs=pl.BlockSpec((CHUNK, nc, wl), lambda i: (0, 0, i)),
        out_shape=jax.ShapeDtypeStruct((CHUNK, nc, e), BF16),
        scratch_shapes=[pltpu.VMEM((CHUNK, wl, nc), BF16),
                        pltpu.VMEM((CHUNK, wl, nc), F32)],
        compiler_params=_cparams(),
        name="ssm_mixer",
    )(uz3, tab, coef, d_all)
    return out.reshape(t, e)


def _ssm_tables(a_re, a_im, log_dt, b_re, b_im, c_re, c_im, seg_tokens):
    dt = jnp.exp(log_dt)[:, None]
    mag = jnp.exp(a_re * dt)
    abr = mag * jnp.cos(a_im * dt)
    abi = mag * jnp.sin(a_im * dt)
    den = a_re * a_re + a_im * a_im
    nr = abr - 1.0
    fr = (nr * a_re + abi * a_im) / den
    fi = (abi * a_re - nr * a_im) / den
    bbr = fr[..., None] * b_re - fi[..., None] * b_im
    bbi = fr[..., None] * b_im + fi[..., None] * b_re

    def powers(r, i, count):
        prs, pis = [jnp.ones_like(r)], [jnp.zeros_like(r)]
        for _ in range(count):
            pr, pi = prs[-1], pis[-1]
            prs.append(pr * r - pi * i)
            pis.append(pr * i + pi * r)
        return jnp.stack(prs, axis=1), jnp.stack(pis, axis=1)

    pr, pi = powers(abr, abi, CHUNK)
    mag2 = abr * abr + abi * abi
    qr, qi = powers(abr / mag2, -abi / mag2, CHUNK - 1)

    def cat(x, y):
        return jnp.concatenate([x, y], axis=-1)

    def pad(x, rows):
        return jnp.pad(x, ((0, 0), (0, rows - x.shape[1]), (0, 0)))

    er, ei = pr[:, CHUNK - 1::-1], pi[:, CHUNK - 1::-1]
    bt_r, bt_i = bbr.transpose(0, 2, 1), bbi.transpose(0, 2, 1)
    tab = jnp.concatenate([
        cat(er, er), cat(-ei, ei),
        cat(qr, qr), cat(-qi, qi),
        pad(cat(pr, -pi), _T_D2 - _T_D1),
        pad(cat(-pi, -pr), _T_BP - _T_D2),
        cat(bt_r, bt_i), cat(bt_i, bt_r),
        cat(c_re, c_re), cat(c_im, c_im),
    ], axis=1)

    r, i = pr[:, CHUNK], pi[:, CHUNK]
    cols = [r, i]
    reach = CHUNK
    while reach < seg_tokens:
        r, i = r * r - i * i, 2.0 * r * i
        reach *= 2
    while len(cols) < 16:
        cols.extend([r, i])
        r, i = r * r - i * i, 2.0 * r * i
    coef = jnp.stack(cols, axis=-1)
    return tab, coef


def _glu_kernel(a_ref, w_ref, b_ref, g_ref, z_ref, o_ref, wb_ref):
    @pl.when(pl.program_id(1) == 0)
    def _():
        wb_ref[...] = w_ref[...].astype(BF16)

    lin = jnp.dot(a_ref[...], wb_ref[...], preferred_element_type=F32) + b_ref[...]
    gg = g_ref[...].astype(F32)
    o_ref[...] = (gg * jax.nn.sigmoid(lin) * _silu(z_ref[...].astype(F32))).astype(o_ref.dtype)


def _glu(gact, w_all, b_all, uz, j, tm=512, tn=512):
    t, e = gact.shape
    zoff = e // tn
    return pl.pallas_call(
        _glu_kernel,
        grid=(e // tn, t // tm),
        in_specs=[pl.BlockSpec((tm, e), lambda c, r: (r, 0)),
                  pl.BlockSpec((None, e, tn), lambda c, r: (j, 0, c)),
                  pl.BlockSpec((None, 1, tn), lambda c, r: (j, 0, c)),
                  pl.BlockSpec((tm, tn), lambda c, r: (r, c)),
                  pl.BlockSpec((tm, tn), lambda c, r: (r, c + zoff))],
        out_specs=pl.BlockSpec((tm, tn), lambda c, r: (r, c)),
        out_shape=jax.ShapeDtypeStruct((t, e), BF16),
        scratch_shapes=[pltpu.VMEM((e, tn), BF16)],
        compiler_params=_cparams(),
        name="glu",
    )(gact, w_all, b_all, gact, uz)


def _out_proj_kernel(y_ref, w_ref, x_ref, g_ref, *o_refs, last):
    xn = x_ref[...] + jnp.dot(y_ref[...], w_ref[...], preferred_element_type=F32)
    inv = lax.rsqrt(jnp.mean(xn * xn, axis=-1, keepdims=True) + NORM_EPS)
    hn = xn * inv * g_ref[...]
    if last:
        o_refs[0][...] = hn
    else:
        o_refs[0][...] = xn
        o_refs[1][...] = hn.astype(o_refs[1].dtype)


def _out_proj(y, w_all, j, x, gains, layer, last, tm=256):
    t, e = y.shape
    d = w_all.shape[2]
    row = pl.BlockSpec((tm, d), lambda i: (i, 0))
    if last:
        out_shape = jax.ShapeDtypeStruct((t, d), F32)
        out_specs = row
    else:
        out_shape = (jax.ShapeDtypeStruct((t, d), F32), jax.ShapeDtypeStruct((t, d), BF16))
        out_specs = (row, row)
    return pl.pallas_call(
        functools.partial(_out_proj_kernel, last=last),
        grid=(t // tm,),
        in_specs=[pl.BlockSpec((tm, e), lambda i: (i, 0)),
                  pl.BlockSpec((None, e, d), lambda i: (j, 0, 0)),
                  row,
                  pl.BlockSpec((None, 1, d), lambda i: (layer, 0, 0))],
        out_specs=out_specs,
        out_shape=out_shape,
        compiler_params=_cparams(),
        name="out_proj",
    )(y, w_all, x, gains)


def kernel(x, norm_g, final_norm_g, pool_w_in, pool_w_grp, pool_scale, pool_w_out, ssm_w_in, ssm_a_re, ssm_a_im, ssm_log_dt, ssm_b_re, ssm_b_im, ssm_c_re, ssm_c_im, ssm_d, ssm_w_glu, ssm_b_glu, ssm_w_out):
    nb, seq_len, d = x.shape
    depth = norm_g.shape[0]
    segs = LANES // nb
    ni = seq_len // (CHUNK * segs)
    assert nb * segs == LANES and ni * segs * CHUNK == seq_len
    t = nb * seq_len

    xs = x.reshape(nb, segs, ni, CHUNK, d).transpose(3, 2, 0, 1, 4).reshape(t, d)
    gains = jnp.concatenate([norm_g, final_norm_g[None]], axis=0)[:, None, :]
    pool_scale3 = pool_scale[:, None, :]
    ssm_d3 = ssm_d[:, None, :]
    ssm_b_glu3 = ssm_b_glu[:, None, :]
    pool_w_out_b = pool_w_out.astype(BF16)
    ssm_w_out_b = ssm_w_out.astype(BF16)

    h = _rmsnorm(xs, gains, 0)
    for layer in range(depth):
        j = layer // 2
        if layer % 2 == 0:
            uz = _in_proj(h, pool_w_in, j)
            y = _pool_mixer(uz, pool_w_grp, pool_scale3, j, nb, ni)
            w_out = pool_w_out_b
        else:
            uz = _in_proj(h, ssm_w_in, j)
            tab, coef = _ssm_tables(ssm_a_re[j], ssm_a_im[j], ssm_log_dt[j], ssm_b_re[j],
                                    ssm_b_im[j], ssm_c_re[j], ssm_c_im[j], ni * CHUNK)
            gact = _ssm_mixer(uz, tab, coef, ssm_d3, j, nb, ni)
            y = _glu(gact, ssm_w_glu, ssm_b_glu3, uz, j)
            w_out = ssm_w_out_b
        last = layer == depth - 1
        res = _out_proj(y, w_out, j, xs, gains, layer + 1, last)
        if last:
            return res.reshape(CHUNK, ni, nb, segs, d).transpose(2, 3, 1, 0, 4).reshape(nb, seq_len, d)
        xs, h = res
```

```python
import functools

import jax
import jax.numpy as jnp
from jax import lax
from jax.experimental import pallas as pl
from jax.experimental.pallas import tpu as pltpu

F32 = jnp.float32
BF16 = jnp.bfloat16

POOL_WINDOWS = (2, 4, 8, 16)
SSM_P = 16
SSM_N = 64
CHUNK = 16
LANES = 128
NORM_EPS = 1e-6
VMEM_LIMIT = 56 * 1024 * 1024

_T_F1, _T_F2, _T_G1, _T_G2, _T_D1, _T_D2, _T_BP, _T_BQ, _T_CR, _T_CI, _T_ROWS = (
    0, 16, 32, 48, 64, 88, 112, 128, 144, 160, 176)


def _cparams():
    return pltpu.CompilerParams(vmem_limit_bytes=VMEM_LIMIT)


def _silu(z):
    return z * jax.nn.sigmoid(z)


def _rmsnorm_kernel(x_ref, g_ref, o_ref):
    x = x_ref[...]
    inv = lax.rsqrt(jnp.mean(x * x, axis=-1, keepdims=True) + NORM_EPS)
    o_ref[...] = (x * inv * g_ref[...]).astype(o_ref.dtype)


def _rmsnorm(x, gains, layer, tm=512):
    t, d = x.shape
    return pl.pallas_call(
        _rmsnorm_kernel,
        grid=(t // tm,),
        in_specs=[pl.BlockSpec((tm, d), lambda i: (i, 0)),
                  pl.BlockSpec((None, 1, d), lambda i: (layer, 0, 0))],
        out_specs=pl.BlockSpec((tm, d), lambda i: (i, 0)),
        out_shape=jax.ShapeDtypeStruct((t, d), BF16),
        compiler_params=_cparams(),
        name="rmsnorm",
    )(x, gains)


def _in_proj_kernel(a_ref, w_ref, o_ref, wb_ref):
    @pl.when(pl.program_id(1) == 0)
    def _():
        wb_ref[...] = w_ref[...].astype(BF16)

    o_ref[...] = jnp.dot(a_ref[...], wb_ref[...],
                         preferred_element_type=F32).astype(o_ref.dtype)


def _in_proj(h, w_all, j, tm=1024, tn=1024):
    t, k = h.shape
    n = w_all.shape[2]
    return pl.pallas_call(
        _in_proj_kernel,
        grid=(n // tn, t // tm),
        in_specs=[pl.BlockSpec((tm, k), lambda c, r: (r, 0)),
                  pl.BlockSpec((None, k, tn), lambda c, r: (j, 0, c))],
        out_specs=pl.BlockSpec((tm, tn), lambda c, r: (r, c)),
        out_shape=jax.ShapeDtypeStruct((t, n), BF16),
        scratch_shapes=[pltpu.VMEM((k, tn), BF16)],
        compiler_params=_cparams(),
        name="in_proj",
    )(h, w_all)


def _pool_kernel(u_ref, prev_ref, z_ref, wg_ref, sc_ref, o_ref, wb_ref, cur_ref, hist_ref, p_ref,
                 *, rows, cg, sub):
    g = pl.program_id(0)
    i = pl.program_id(1)
    b = pl.program_id(2)

    @pl.when((i == 0) & (b == 0))
    def _():
        wb_ref[...] = wg_ref[...].astype(BF16)

    seg0 = lax.broadcasted_iota(jnp.int32, (rows, 1), 0) == 0
    for m in range(1, CHUNK):
        blk = prev_ref[CHUNK - m].astype(F32)
        moved = jnp.where(seg0, 0.0, pltpu.roll(blk, 1, axis=0))
        hist_ref[m - 1] = jnp.where(i == 0, moved, blk)
    for t in range(CHUNK):
        cur_ref[t] = u_ref[t].astype(F32)

    n_r, n_c = rows // 8, cg // LANES
    for gi, w in enumerate(POOL_WINDOWS):
        @pl.when(g == gi)
        def _(w=w):
            def strip(idx, carry):
                r0 = pl.multiple_of((idx // n_c) * 8, 8)
                c0 = pl.multiple_of((idx % n_c) * LANES, LANES)
                rs, cs = pl.ds(r0, 8), pl.ds(c0, LANES)
                cur = [cur_ref[t, rs, cs] for t in range(CHUNK)]
                s = {t: cur[t] for t in range(CHUNK)}
                for m in range(1, w):
                    s[-m] = hist_ref[m - 1, rs, cs]
                lo, k = -(w - 1), 1
                while k < w:
                    s = {t: s[t] + s[t - k] for t in range(lo + k, CHUNK)}
                    lo += k
                    k *= 2
                first = (i == 0) & (lax.broadcasted_iota(jnp.int32, (8, 1), 0) + r0 == 0)
                for t in range(CHUNK):
                    inv = jnp.where(first, 1.0 / min(t + 1, w), 1.0 / w)
                    p_ref[t, rs, cs] = s[t] * inv - cur[t]
                return carry

            lax.fori_loop(0, n_r * n_c, strip, 0)

    for q in range(CHUNK // sub):
        ts = slice(q * sub, (q + 1) * sub)
        p = p_ref[ts].reshape(sub * rows, cg).astype(BF16)
        m = jnp.dot(p, wb_ref[...], preferred_element_type=F32)
        zz = z_ref[ts].reshape(sub * rows, cg).astype(F32)
        y = m * sc_ref[...] * _silu(zz)
        o_ref[ts] = y.astype(o_ref.dtype).reshape(sub, rows, cg)


def _pool_mixer(uz, w_grp_all, scale_all, j, nb, ni):
    t, e2 = uz.shape
    e = e2 // 2
    ng, cg = w_grp_all.shape[1], w_grp_all.shape[2]
    nc = t // CHUNK
    rows = nc // (ni * nb)
    uz3 = uz.reshape(CHUNK, nc, e2)
    blk = (CHUNK, rows, cg)
    kern = functools.partial(_pool_kernel, rows=rows, cg=cg, sub=4)
    out = pl.pallas_call(
        kern,
        grid=(ng, ni, nb),
        in_specs=[pl.BlockSpec(blk, lambda g, i, b: (0, i * nb + b, g)),
                  pl.BlockSpec(blk, lambda g, i, b: (0, ((i + ni - 1) % ni) * nb + b, g)),
                  pl.BlockSpec(blk, lambda g, i, b: (0, i * nb + b, ng + g)),
                  pl.BlockSpec((None, None, cg, cg), lambda g, i, b: (j, g, 0, 0)),
                  pl.BlockSpec((None, 1, cg), lambda g, i, b: (j, 0, g))],
        out_specs=pl.BlockSpec(blk, lambda g, i, b: (0, i * nb + b, g)),
        out_shape=jax.ShapeDtypeStruct((CHUNK, nc, e), BF16),
        scratch_shapes=[pltpu.VMEM((cg, cg), BF16),
                        pltpu.VMEM((CHUNK, rows, cg), F32),
                        pltpu.VMEM((CHUNK - 1, rows, cg), F32),
                        pltpu.VMEM((CHUNK, rows, cg), F32)],
        compiler_params=_cparams(),
        name="pool_mixer",
    )(uz3, uz3, uz3, w_grp_all, scale_all)
    return out.reshape(t, e)


def _gen_group_weights(tab):
    def row(r):
        return jnp.broadcast_to(tab[r:r + 1, :], (SSM_P, 2 * SSM_N))

    bp, bq = tab[_T_BP:_T_BP + SSM_P], tab[_T_BQ:_T_BQ + SSM_P]
    cr, ci = tab[_T_CR:_T_CR + SSM_P], tab[_T_CI:_T_CI + SSM_P]
    wb = jnp.concatenate([bp * row(_T_F1 + t) + bq * row(_T_F2 + t) for t in range(CHUNK)], axis=0)
    rn = jnp.concatenate([bp * row(_T_G1 + t) + bq * row(_T_G2 + t) for t in range(CHUNK)], axis=0)
    cb = [cr * row(_T_D1 + d) + ci * row(_T_D2 + d) for d in range(CHUNK + 1)]
    lm = jnp.concatenate(cb[:CHUNK], axis=0)
    wc = jnp.concatenate(cb[1:], axis=0)
    lm_h, rn_h = lm.astype(BF16), rn.astype(BF16)
    lm_l = (lm - lm_h.astype(F32)).astype(BF16)
    rn_l = (rn - rn_h.astype(F32)).astype(BF16)
    nt = (((1,), (1,)), ((), ()))
    kt = (lax.dot_general(lm_h, rn_h, nt, preferred_element_type=F32)
          + lax.dot_general(lm_h, rn_l, nt, preferred_element_type=F32)
          + lax.dot_general(lm_l, rn_h, nt, preferred_element_type=F32))
    n = CHUNK * SSM_P
    r_tau = lax.broadcasted_iota(jnp.int32, (n, n), 0) // SSM_P
    c_tau = lax.broadcasted_iota(jnp.int32, (n, n), 1) // SSM_P
    kt = jnp.where(r_tau >= c_tau, kt, 0.0)
    return wb.T.astype(BF16), wc.astype(BF16), kt.astype(BF16)


def _ssm_kernel(u_ref, tab_ref, co_ref, c1_ref, c2_ref, d_ref, o_ref, xt_ref, ot_ref, wc_ref,
                kt_ref, st_ref, a_ref, ep_ref, eq_ref, hp_ref, *, gb, nc, ni, segs):
    n = SSM_N
    kk = CHUNK * SSM_P
    gl = LANES

    for t in range(CHUNK):
        xt_ref[t] = u_ref[t].astype(F32).T.astype(BF16)

    def cmul(ar, ai, xr, xi):
        return ar * xr - ai * xi, ar * xi + ai * xr

    def group_rows(j):
        return pl.ds(pl.multiple_of(j * SSM_P, SSM_P), SSM_P)

    def scan_rows(j):
        return pl.ds(pl.multiple_of(j * gl, gl), gl)

    def tiles(st):
        return ([st[:n, i * LANES:(i + 1) * LANES] for i in range(ni)],
                [st[n:, i * LANES:(i + 1) * LANES] for i in range(ni)])

    def head(j, carry):
        ut = xt_ref[:, group_rows(j), :].reshape(kk, nc)
        wbt, wct, ktt = _gen_group_weights(tab_ref[j])
        wc_ref[j] = wct
        kt_ref[j] = ktt
        st = jnp.dot(wbt, ut, preferred_element_type=F32)
        st_ref[j] = st
        s_r, s_i = tiles(st)
        co = co_ref[j]
        ar = jnp.broadcast_to(co[:, 0:1], (n, LANES))
        ai = jnp.broadcast_to(co[:, 1:2], (n, LANES))
        a_ref[j] = jnp.concatenate([ar, ai], axis=0)
        er, ei = s_r[0], s_i[0]
        for i in range(1, ni):
            mr, mi = cmul(ar, ai, er, ei)
            er, ei = mr + s_r[i], mi + s_i[i]
        ep_ref[scan_rows(j), :] = jnp.concatenate([er, ei], axis=0).T
        eq_ref[scan_rows(j), :] = jnp.concatenate([ei, er], axis=0).T
        return carry

    lax.fori_loop(0, gb, head, 0, unroll=2)

    c1, c2 = c1_ref[...], c2_ref[...]
    pairs = gb * (gl // segs)

    def seg_step(s, carry):
        p, q = carry
        at = pl.ds(s, pairs, stride=segs)
        hp_ref[at, :] = p
        return (c1 * p + c2 * q + ep_ref[at, :], c1 * q - c2 * p + eq_ref[at, :])

    zero = jnp.zeros((pairs, 2 * n), F32)
    lax.fori_loop(0, segs, seg_step, (zero, zero), unroll=8)

    def tail(j, carry):
        rows = group_rows(j)
        ut = xt_ref[:, rows, :].reshape(kk, nc)
        s_r, s_i = tiles(st_ref[j])
        a = a_ref[j]
        ar, ai = a[:n], a[n:]
        e = hp_ref[scan_rows(j), :].T
        hr, hi = e[:n], e[n:]
        h_tiles = []
        for i in range(ni):
            h_tiles.append(jnp.concatenate([hr, hi], axis=0).astype(BF16))
            if i + 1 < ni:
                mr, mi = cmul(ar, ai, hr, hi)
                hr, hi = mr + s_r[i], mi + s_i[i]
        ht = jnp.concatenate(h_tiles, axis=1)
        yt = jnp.dot(kt_ref[j], ut, preferred_element_type=F32)
        yt = yt + jnp.dot(wc_ref[j], ht, preferred_element_type=F32)
        ot_ref[:, rows, :] = yt.reshape(CHUNK, SSM_P, nc)
        return carry

    lax.fori_loop(0, gb, tail, 0, unroll=2)

    for t in range(CHUNK):
        y = ot_ref[t].T + d_ref[...] * u_ref[t].astype(F32)
        o_ref[t] = jax.nn.gelu(y).astype(o_ref.dtype)


def _ssm_mixer(uz, tab, coef, c1s, c2s, d_all, j, nb, ni, gb=8):
    t, e2 = uz.shape
    e = e2 // 2
    g = e // SSM_P
    nc = t // CHUNK
    segs = LANES // nb
    uz3 = uz.reshape(CHUNK, nc, e2)
    wl = gb * SSM_P
    kern = functools.partial(_ssm_kernel, gb=gb, nc=nc, ni=ni, segs=segs)
    out = pl.pallas_call(
        kern,
        grid=(g // gb,),
        in_specs=[pl.BlockSpec((CHUNK, nc, wl), lambda i: (0, 0, i)),
                  pl.BlockSpec((gb, _T_ROWS, 2 * SSM_N), lambda i: (i, 0, 0)),
                  pl.BlockSpec((gb, SSM_N, 2), lambda i: (i, 0, 0)),
                  pl.BlockSpec((gb * nb, 2 * SSM_N), lambda i: (i, 0)),
                  pl.BlockSpec((gb * nb, 2 * SSM_N), lambda i: (i, 0)),
                  pl.BlockSpec((None, 1, wl), lambda i: (j, 0, i))],
        out_specs=pl.BlockSpec((CHUNK, nc, wl), lambda i: (0, 0, i)),
        out_shape=jax.ShapeDtypeStruct((CHUNK, nc, e), BF16),
        scratch_shapes=[pltpu.VMEM((CHUNK, wl, nc), BF16),
                        pltpu.VMEM((CHUNK, wl, nc), F32),
                        pltpu.VMEM((gb, CHUNK * SSM_P, 2 * SSM_N), BF16),
                        pltpu.VMEM((gb, CHUNK * SSM_P, CHUNK * SSM_P), BF16),
                        pltpu.VMEM((gb, 2 * SSM_N, nc), F32),
                        pltpu.VMEM((gb, 2 * SSM_N, LANES), F32),
                        pltpu.VMEM((gb * LANES, 2 * SSM_N), F32),
                        pltpu.VMEM((gb * LANES, 2 * SSM_N), F32),
                        pltpu.VMEM((gb * LANES, 2 * SSM_N), F32)],
        compiler_params=_cparams(),
        name="ssm_mixer",
    )(uz3, tab, coef, c1s, c2s, d_all)
    return out.reshape(t, e)


def _ssm_tables(a_re, a_im, log_dt, b_re, b_im, c_re, c_im, seg_tokens, nb):
    dt = jnp.exp(log_dt)[:, None]
    mag = jnp.exp(a_re * dt)
    abr = mag * jnp.cos(a_im * dt)
    abi = mag * jnp.sin(a_im * dt)
    den = a_re * a_re + a_im * a_im
    nr = abr - 1.0
    fr = (nr * a_re + abi * a_im) / den
    fi = (abi * a_re - nr * a_im) / den
    bbr = fr[..., None] * b_re - fi[..., None] * b_im
    bbi = fr[..., None] * b_im + fi[..., None] * b_re

    def powers(r, i, count):
        prs, pis = [jnp.ones_like(r)], [jnp.zeros_like(r)]
        for _ in range(count):
            pr, pi = prs[-1], pis[-1]
            prs.append(pr * r - pi * i)
            pis.append(pr * i + pi * r)
        return jnp.stack(prs, axis=1), jnp.stack(pis, axis=1)

    pr, pi = powers(abr, abi, CHUNK)
    mag2 = abr * abr + abi * abi
    qr, qi = powers(abr / mag2, -abi / mag2, CHUNK - 1)

    def cat(x, y):
        return jnp.concatenate([x, y], axis=-1)

    def pad(x, rows):
        return jnp.pad(x, ((0, 0), (0, rows - x.shape[1]), (0, 0)))

    er, ei = pr[:, CHUNK - 1::-1], pi[:, CHUNK - 1::-1]
    bt_r, bt_i = bbr.transpose(0, 2, 1), bbi.transpose(0, 2, 1)
    tab = jnp.concatenate([
        cat(er, er), cat(-ei, ei),
        cat(qr, qr), cat(-qi, qi),
        pad(cat(pr, -pi), _T_D2 - _T_D1),
        pad(cat(-pi, -pr), _T_BP - _T_D2),
        cat(bt_r, bt_i), cat(bt_i, bt_r),
        cat(c_re, c_re), cat(c_im, c_im),
    ], axis=1)

    r, i = pr[:, CHUNK], pi[:, CHUNK]
    coef = jnp.stack([r, i], axis=-1)
    reach = CHUNK
    while reach < seg_tokens:
        r, i = r * r - i * i, 2.0 * r * i
        reach *= 2
    c1s = jnp.repeat(cat(r, r), nb, axis=0)
    c2s = jnp.repeat(cat(-i, i), nb, axis=0)
    return tab, coef, c1s, c2s


def _glu_kernel(a_ref, w_ref, b_ref, g_ref, z_ref, o_ref, wb_ref):
    @pl.when(pl.program_id(1) == 0)
    def _():
        wb_ref[...] = w_ref[...].astype(BF16)

    lin = jnp.dot(a_ref[...], wb_ref[...], preferred_element_type=F32) + b_ref[...]
    gg = g_ref[...].astype(F32)
    o_ref[...] = (gg * jax.nn.sigmoid(lin) * _silu(z_ref[...].astype(F32))).astype(o_ref.dtype)


def _glu(gact, w_all, b_all, uz, j, tm=512, tn=512):
    t, e = gact.shape
    zoff = e // tn
    return pl.pallas_call(
        _glu_kernel,
        grid=(e // tn, t // tm),
        in_specs=[pl.BlockSpec((tm, e), lambda c, r: (r, 0)),
                  pl.BlockSpec((None, e, tn), lambda c, r: (j, 0, c)),
                  pl.BlockSpec((None, 1, tn), lambda c, r: (j, 0, c)),
                  pl.BlockSpec((tm, tn), lambda c, r: (r, c)),
                  pl.BlockSpec((tm, tn), lambda c, r: (r, c + zoff))],
        out_specs=pl.BlockSpec((tm, tn), lambda c, r: (r, c)),
        out_shape=jax.ShapeDtypeStruct((t, e), BF16),
        scratch_shapes=[pltpu.VMEM((e, tn), BF16)],
        compiler_params=_cparams(),
        name="glu",
    )(gact, w_all, b_all, gact, uz)


def _out_proj_kernel(y_ref, w_ref, x_ref, g_ref, *o_refs, last):
    xn = x_ref[...] + jnp.dot(y_ref[...], w_ref[...], preferred_element_type=F32)
    inv = lax.rsqrt(jnp.mean(xn * xn, axis=-1, keepdims=True) + NORM_EPS)
    hn = xn * inv * g_ref[...]
    if last:
        o_refs[0][...] = hn
    else:
        o_refs[0][...] = xn
        o_refs[1][...] = hn.astype(o_refs[1].dtype)


def _out_proj(y, w_all, j, x, gains, layer, last, tm=256):
    t, e = y.shape
    d = w_all.shape[2]
    row = pl.BlockSpec((tm, d), lambda i: (i, 0))
    if last:
        out_shape = jax.ShapeDtypeStruct((t, d), F32)
        out_specs = row
    else:
        out_shape = (jax.ShapeDtypeStruct((t, d), F32), jax.ShapeDtypeStruct((t, d), BF16))
        out_specs = (row, row)
    return pl.pallas_call(
        functools.partial(_out_proj_kernel, last=last),
        grid=(t // tm,),
        in_specs=[pl.BlockSpec((tm, e), lambda i: (i, 0)),
                  pl.BlockSpec((None, e, d), lambda i: (j, 0, 0)),
                  row,
                  pl.BlockSpec((None, 1, d), lambda i: (layer, 0, 0))],
        out_specs=out_specs,
        out_shape=out_shape,
        compiler_params=_cparams(),
        name="out_proj",
    )(y, w_all, x, gains)


def kernel(x, norm_g, final_norm_g, pool_w_in, pool_w_grp, pool_scale, pool_w_out, ssm_w_in, ssm_a_re, ssm_a_im, ssm_log_dt, ssm_b_re, ssm_b_im, ssm_c_re, ssm_c_im, ssm_d, ssm_w_glu, ssm_b_glu, ssm_w_out):
    nb, seq_len, d = x.shape
    depth = norm_g.shape[0]
    segs = LANES // nb
    ni = seq_len // (CHUNK * segs)
    assert nb * segs == LANES and ni * segs * CHUNK == seq_len
    t = nb * seq_len

    xs = x.reshape(nb, segs, ni, CHUNK, d).transpose(3, 2, 0, 1, 4).reshape(t, d)
    gains = jnp.concatenate([norm_g, final_norm_g[None]], axis=0)[:, None, :]
    pool_scale3 = pool_scale[:, None, :]
    ssm_d3 = ssm_d[:, None, :]
    ssm_b_glu3 = ssm_b_glu[:, None, :]
    pool_w_out_b = pool_w_out.astype(BF16)
    ssm_w_out_b = ssm_w_out.astype(BF16)

    h = _rmsnorm(xs, gains, 0)
    for layer in range(depth):
        j = layer // 2
        if layer % 2 == 0:
            uz = _in_proj(h, pool_w_in, j)
            y = _pool_mixer(uz, pool_w_grp, pool_scale3, j, nb, ni)
            w_out = pool_w_out_b
        else:
            uz = _in_proj(h, ssm_w_in, j)
            tab, coef, c1s, c2s = _ssm_tables(
                ssm_a_re[j], ssm_a_im[j], ssm_log_dt[j], ssm_b_re[j], ssm_b_im[j],
                ssm_c_re[j], ssm_c_im[j], ni * CHUNK, nb)
            gact = _ssm_mixer(uz, tab, coef, c1s, c2s, ssm_d3, j, nb, ni)
            y = _glu(gact, ssm_w_glu, ssm_b_glu3, uz, j)
            w_out = ssm_w_out_b
        last = layer == depth - 1
        res = _out_proj(y, w_out, j, xs, gains, layer + 1, last)
        if last:
            return res.reshape(CHUNK, ni, nb, segs, d).transpose(2, 3, 1, 0, 4).reshape(nb, seq_len, d)
        xs, h = res
```

```python
import functools
import math

import jax
import jax.numpy as jnp
from jax import lax
from jax.experimental import pallas as pl
from jax.experimental.pallas import tpu as pltpu

F32 = jnp.float32
BF16 = jnp.bfloat16

POOL_WINDOWS = (2, 4, 8, 16)
SSM_P = 16
SSM_N = 64
CHUNK = 16
LANES = 128
NORM_EPS = 1e-6
VMEM_LIMIT = 56 * 1024 * 1024

_T_F1, _T_F2, _T_G1, _T_G2, _T_D1, _T_D2, _T_BP, _T_BQ, _T_CR, _T_CI, _T_ROWS = (
    0, 16, 32, 48, 64, 88, 112, 128, 144, 160, 176)


def _cparams():
    return pltpu.CompilerParams(vmem_limit_bytes=VMEM_LIMIT)


def _silu(z):
    return z * jax.nn.sigmoid(z)


def _gelu_tanh(y):
    a = -2.0 * math.sqrt(2.0 / math.pi) * math.log2(math.e)
    return y / (1.0 + jnp.exp2(y * (a + (a * 0.044715) * (y * y))))


def _rmsnorm_kernel(x_ref, g_ref, o_ref):
    x = x_ref[...]
    inv = lax.rsqrt(jnp.mean(x * x, axis=-1, keepdims=True) + NORM_EPS)
    o_ref[...] = (x * inv * g_ref[...]).astype(o_ref.dtype)


def _rmsnorm(x, gains, layer, tm=512):
    t, d = x.shape
    return pl.pallas_call(
        _rmsnorm_kernel,
        grid=(t // tm,),
        in_specs=[pl.BlockSpec((tm, d), lambda i: (i, 0)),
                  pl.BlockSpec((None, 1, d), lambda i: (layer, 0, 0))],
        out_specs=pl.BlockSpec((tm, d), lambda i: (i, 0)),
        out_shape=jax.ShapeDtypeStruct((t, d), BF16),
        compiler_params=_cparams(),
        name="rmsnorm",
    )(x, gains)


def _in_proj_kernel(a_ref, w_ref, o_ref, wb_ref):
    @pl.when(pl.program_id(1) == 0)
    def _():
        wb_ref[...] = w_ref[...].astype(BF16)

    o_ref[...] = jnp.dot(a_ref[...], wb_ref[...],
                         preferred_element_type=F32).astype(o_ref.dtype)


def _in_proj(h, w_all, j, tm=1024, tn=1024):
    t, k = h.shape
    n = w_all.shape[2]
    return pl.pallas_call(
        _in_proj_kernel,
        grid=(n // tn, t // tm),
        in_specs=[pl.BlockSpec((tm, k), lambda c, r: (r, 0)),
                  pl.BlockSpec((None, k, tn), lambda c, r: (j, 0, c))],
        out_specs=pl.BlockSpec((tm, tn), lambda c, r: (r, c)),
        out_shape=jax.ShapeDtypeStruct((t, n), BF16),
        scratch_shapes=[pltpu.VMEM((k, tn), BF16)],
        compiler_params=_cparams(),
        name="in_proj",
    )(h, w_all)


def _pool_kernel(u_ref, prev_ref, z_ref, wg_ref, sc_ref, o_ref, wb_ref, cur_ref, hist_ref, p_ref,
                 *, rows, cg, sub):
    g = pl.program_id(0)
    i = pl.program_id(1)
    b = pl.program_id(2)

    @pl.when((i == 0) & (b == 0))
    def _():
        wb_ref[...] = wg_ref[...].astype(BF16)

    seg0 = lax.broadcasted_iota(jnp.int32, (rows, 1), 0) == 0
    for t in range(CHUNK):
        cur_ref[t] = u_ref[t].astype(F32)

    n_r, n_c = rows // 8, cg // LANES
    for gi, w in enumerate(POOL_WINDOWS):
        @pl.when(g == gi)
        def _(w=w):
            @pl.when(i == 0)
            def _():
                for m in range(1, w):
                    blk = prev_ref[CHUNK - m].astype(F32)
                    hist_ref[m - 1] = jnp.where(seg0, 0.0, pltpu.roll(blk, 1, axis=0))

            @pl.when(i != 0)
            def _():
                for m in range(1, w):
                    hist_ref[m - 1] = prev_ref[CHUNK - m].astype(F32)

            def strip(idx, carry):
                r0 = pl.multiple_of((idx // n_c) * 8, 8)
                c0 = pl.multiple_of((idx % n_c) * LANES, LANES)
                rs, cs = pl.ds(r0, 8), pl.ds(c0, LANES)
                cur = [cur_ref[t, rs, cs] for t in range(CHUNK)]
                s = {t: cur[t] for t in range(CHUNK)}
                for m in range(1, w):
                    s[-m] = hist_ref[m - 1, rs, cs]
                lo, k = -(w - 1), 1
                while k < w:
                    s = {t: s[t] + s[t - k] for t in range(lo + k, CHUNK)}
                    lo += k
                    k *= 2
                first = (i == 0) & (lax.broadcasted_iota(jnp.int32, (8, 1), 0) + r0 == 0)
                for t in range(CHUNK):
                    inv = jnp.where(first, 1.0 / min(t + 1, w), 1.0 / w)
                    p_ref[t, rs, cs] = s[t] * inv - cur[t]
                return carry

            lax.fori_loop(0, n_r * n_c, strip, 0)

    for q in range(CHUNK // sub):
        ts = slice(q * sub, (q + 1) * sub)
        p = p_ref[ts].reshape(sub * rows, cg).astype(BF16)
        m = jnp.dot(p, wb_ref[...], preferred_element_type=F32)
        zz = z_ref[ts].reshape(sub * rows, cg).astype(F32)
        y = m * sc_ref[...] * _silu(zz)
        o_ref[ts] = y.astype(o_ref.dtype).reshape(sub, rows, cg)


def _pool_mixer(uz, w_grp_all, scale_all, j, nb, ni):
    t, e2 = uz.shape
    e = e2 // 2
    ng, cg = w_grp_all.shape[1], w_grp_all.shape[2]
    nc = t // CHUNK
    rows = nc // (ni * nb)
    uz3 = uz.reshape(CHUNK, nc, e2)
    blk = (CHUNK, rows, cg)
    kern = functools.partial(_pool_kernel, rows=rows, cg=cg, sub=4)
    out = pl.pallas_call(
        kern,
        grid=(ng, ni, nb),
        in_specs=[pl.BlockSpec(blk, lambda g, i, b: (0, i * nb + b, g)),
                  pl.BlockSpec(blk, lambda g, i, b: (0, ((i + ni - 1) % ni) * nb + b, g)),
                  pl.BlockSpec(blk, lambda g, i, b: (0, i * nb + b, ng + g)),
                  pl.BlockSpec((None, None, cg, cg), lambda g, i, b: (j, g, 0, 0)),
                  pl.BlockSpec((None, 1, cg), lambda g, i, b: (j, 0, g))],
        out_specs=pl.BlockSpec(blk, lambda g, i, b: (0, i * nb + b, g)),
        out_shape=jax.ShapeDtypeStruct((CHUNK, nc, e), BF16),
        scratch_shapes=[pltpu.VMEM((cg, cg), BF16),
                        pltpu.VMEM((CHUNK, rows, cg), F32),
                        pltpu.VMEM((CHUNK - 1, rows, cg), F32),
                        pltpu.VMEM((CHUNK, rows, cg), F32)],
        compiler_params=_cparams(),
        name="pool_mixer",
    )(uz3, uz3, uz3, w_grp_all, scale_all)
    return out.reshape(t, e)


def _causal_mask():
    n = CHUNK * SSM_P
    r_tau = lax.broadcasted_iota(jnp.int32, (n, n), 0) // SSM_P
    c_tau = lax.broadcasted_iota(jnp.int32, (n, n), 1) // SSM_P
    return jnp.where(r_tau >= c_tau, 1.0, 0.0).astype(BF16)


def _gen_group_weights(tab, mask):
    def row(r):
        return jnp.broadcast_to(tab[r:r + 1, :], (SSM_P, 2 * SSM_N))

    bp, bq = tab[_T_BP:_T_BP + SSM_P], tab[_T_BQ:_T_BQ + SSM_P]
    cr, ci = tab[_T_CR:_T_CR + SSM_P], tab[_T_CI:_T_CI + SSM_P]
    wb = jnp.concatenate([bp * row(_T_F1 + t) + bq * row(_T_F2 + t) for t in range(CHUNK)], axis=0)
    rn = jnp.concatenate([bp * row(_T_G1 + t) + bq * row(_T_G2 + t) for t in range(CHUNK)], axis=0)
    cb = [cr * row(_T_D1 + d) + ci * row(_T_D2 + d) for d in range(CHUNK + 1)]
    lm = jnp.concatenate(cb[:CHUNK], axis=0)
    wc = jnp.concatenate(cb[1:], axis=0)
    lm_h, rn_h = lm.astype(BF16), rn.astype(BF16)
    lm_l = (lm - lm_h.astype(F32)).astype(BF16)
    rn_l = (rn - rn_h.astype(F32)).astype(BF16)
    kt = lax.dot_general(jnp.concatenate([lm_h, lm_h, lm_l], axis=1),
                         jnp.concatenate([rn_h, rn_l, rn_h], axis=1),
                         (((1,), (1,)), ((), ())), preferred_element_type=F32)
    return wb.T.astype(BF16), wc.astype(BF16), kt.astype(BF16) * mask


def _ssm_kernel(u_ref, tab_ref, co_ref, c1_ref, c2_ref, d_ref, o_ref, xt_ref, ot_ref, wc_ref,
                kt_ref, st_ref, a_ref, ep_ref, eq_ref, hp_ref, mask_ref, *, gb, nc, ni, segs):
    n = SSM_N
    kk = CHUNK * SSM_P
    gl = LANES

    @pl.when(pl.program_id(0) == 0)
    def _():
        mask_ref[...] = _causal_mask()

    for t in range(CHUNK):
        xt_ref[t] = u_ref[t].astype(F32).T.astype(BF16)

    def cmul(ar, ai, xr, xi):
        return ar * xr - ai * xi, ar * xi + ai * xr

    def group_rows(j):
        return pl.ds(pl.multiple_of(j * SSM_P, SSM_P), SSM_P)

    def scan_rows(j):
        return pl.ds(pl.multiple_of(j * gl, gl), gl)

    def tiles(st):
        return ([st[:n, i * LANES:(i + 1) * LANES] for i in range(ni)],
                [st[n:, i * LANES:(i + 1) * LANES] for i in range(ni)])

    def head(j, carry):
        ut = xt_ref[:, group_rows(j), :].reshape(kk, nc)
        wbt, wct, ktt = _gen_group_weights(tab_ref[j], mask_ref[...])
        wc_ref[j] = wct
        kt_ref[j] = ktt
        st = jnp.dot(wbt, ut, preferred_element_type=F32)
        st_ref[j] = st
        s_r, s_i = tiles(st)
        co = co_ref[j]
        ar = jnp.broadcast_to(co[:, 0:1], (n, LANES))
        ai = jnp.broadcast_to(co[:, 1:2], (n, LANES))
        a_ref[j] = jnp.concatenate([ar, ai], axis=0)
        er, ei = s_r[0], s_i[0]
        for i in range(1, ni):
            mr, mi = cmul(ar, ai, er, ei)
            er, ei = mr + s_r[i], mi + s_i[i]
        ep_ref[scan_rows(j), :] = jnp.concatenate([er, ei], axis=0).T
        eq_ref[scan_rows(j), :] = jnp.concatenate([ei, er], axis=0).T
        return carry

    lax.fori_loop(0, gb, head, 0, unroll=2)

    c1, c2 = c1_ref[...], c2_ref[...]
    pairs = gb * (gl // segs)

    def seg_step(s, carry):
        p, q = carry
        at = pl.ds(s, pairs, stride=segs)
        hp_ref[at, :] = p
        return (c1 * p + c2 * q + ep_ref[at, :], c1 * q - c2 * p + eq_ref[at, :])

    zero = jnp.zeros((pairs, 2 * n), F32)
    lax.fori_loop(0, segs, seg_step, (zero, zero), unroll=8)

    def tail(j, carry):
        rows = group_rows(j)
        ut = xt_ref[:, rows, :].reshape(kk, nc)
        s_r, s_i = tiles(st_ref[j])
        a = a_ref[j]
        ar, ai = a[:n], a[n:]
        e = hp_ref[scan_rows(j), :].T
        hr, hi = e[:n], e[n:]
        h_tiles = []
        for i in range(ni):
            h_tiles.append(jnp.concatenate([hr, hi], axis=0).astype(BF16))
            if i + 1 < ni:
                mr, mi = cmul(ar, ai, hr, hi)
                hr, hi = mr + s_r[i], mi + s_i[i]
        ht = jnp.concatenate(h_tiles, axis=1)
        yt = jnp.dot(kt_ref[j], ut, preferred_element_type=F32)
        yt = yt + jnp.dot(wc_ref[j], ht, preferred_element_type=F32)
        yt = yt + d_ref[j] * ut.astype(F32)
        ot_ref[:, rows, :] = _gelu_tanh(yt).reshape(CHUNK, SSM_P, nc)
        return carry

    lax.fori_loop(0, gb, tail, 0, unroll=2)

    for t in range(CHUNK):
        o_ref[t] = ot_ref[t].T.astype(o_ref.dtype)


def _ssm_mixer(uz, tab, coef, c1s, c2s, d_skip, nb, ni, gb=8):
    t, e2 = uz.shape
    e = e2 // 2
    g = e // SSM_P
    d_col = jnp.tile(d_skip.reshape(g, 1, SSM_P), (1, CHUNK, 1)).reshape(g, CHUNK * SSM_P, 1)
    nc = t // CHUNK
    segs = LANES // nb
    uz3 = uz.reshape(CHUNK, nc, e2)
    wl = gb * SSM_P
    kern = functools.partial(_ssm_kernel, gb=gb, nc=nc, ni=ni, segs=segs)
    out = pl.pallas_call(
        kern,
        grid=(g // gb,),
        in_specs=[pl.BlockSpec((CHUNK, nc, wl), lambda i: (0, 0, i)),
                  pl.BlockSpec((gb, _T_ROWS, 2 * SSM_N), lambda i: (i, 0, 0)),
                  pl.BlockSpec((gb, SSM_N, 2), lambda i: (i, 0, 0)),
                  pl.BlockSpec((gb * nb, 2 * SSM_N), lambda i: (i, 0)),
                  pl.BlockSpec((gb * nb, 2 * SSM_N), lambda i: (i, 0)),
                  pl.BlockSpec((gb, CHUNK * SSM_P, 1), lambda i: (i, 0, 0))],
        out_specs=pl.BlockSpec((CHUNK, nc, wl), lambda i: (0, 0, i)),
        out_shape=jax.ShapeDtypeStruct((CHUNK, nc, e), BF16),
        scratch_shapes=[pltpu.VMEM((CHUNK, wl, nc), BF16),
                        pltpu.VMEM((CHUNK, wl, nc), F32),
                        pltpu.VMEM((gb, CHUNK * SSM_P, 2 * SSM_N), BF16),
                        pltpu.VMEM((gb, CHUNK * SSM_P, CHUNK * SSM_P), BF16),
                        pltpu.VMEM((gb, 2 * SSM_N, nc), F32),
                        pltpu.VMEM((gb, 2 * SSM_N, LANES), F32),
                        pltpu.VMEM((gb * LANES, 2 * SSM_N), F32),
                        pltpu.VMEM((gb * LANES, 2 * SSM_N), F32),
                        pltpu.VMEM((gb * LANES, 2 * SSM_N), F32),
                        pltpu.VMEM((CHUNK * SSM_P, CHUNK * SSM_P), BF16)],
        compiler_params=_cparams(),
        name="ssm_mixer",
    )(uz3, tab, coef, c1s, c2s, d_col)
    return out.reshape(t, e)


def _ssm_tables(a_re, a_im, log_dt, b_re, b_im, c_re, c_im, seg_tokens, nb):
    dt = jnp.exp(log_dt)[:, None]
    mag = jnp.exp(a_re * dt)
    abr = mag * jnp.cos(a_im * dt)
    abi = mag * jnp.sin(a_im * dt)
    den = a_re * a_re + a_im * a_im
    nr = abr - 1.0
    fr = (nr * a_re + abi * a_im) / den
    fi = (abi * a_re - nr * a_im) / den
    bbr = fr[..., None] * b_re - fi[..., None] * b_im
    bbi = fr[..., None] * b_im + fi[..., None] * b_re

    def powers(k):
        kk = k.astype(F32)[None, :, None]
        m = jnp.exp(kk * (a_re * dt)[:, None, :])
        ang = kk * (a_im * dt)[:, None, :]
        return m * jnp.cos(ang), m * jnp.sin(ang)

    pr, pi = powers(jnp.arange(CHUNK + 1))
    qr, qi = powers(-jnp.arange(CHUNK))
    er, ei = powers(CHUNK - 1 - jnp.arange(CHUNK))

    def cat(x, y):
        return jnp.concatenate([x, y], axis=-1)

    def pad(x, rows):
        return jnp.pad(x, ((0, 0), (0, rows - x.shape[1]), (0, 0)))

    bt_r, bt_i = bbr.transpose(0, 2, 1), bbi.transpose(0, 2, 1)
    tab = jnp.concatenate([
        cat(er, er), cat(-ei, ei),
        cat(qr, qr), cat(-qi, qi),
        pad(cat(pr, -pi), _T_D2 - _T_D1),
        pad(cat(-pi, -pr), _T_BP - _T_D2),
        cat(bt_r, bt_i), cat(bt_i, bt_r),
        cat(c_re, c_re), cat(c_im, c_im),
    ], axis=1)

    coef = jnp.stack([pr[:, CHUNK], pi[:, CHUNK]], axis=-1)
    sr, si = powers(jnp.full((1,), seg_tokens))
    r, i = sr[:, 0], si[:, 0]
    c1s = jnp.repeat(cat(r, r), nb, axis=0)
    c2s = jnp.repeat(cat(-i, i), nb, axis=0)
    return tab, coef, c1s, c2s


def _glu_kernel(a_ref, w_ref, b_ref, g_ref, z_ref, o_ref, wb_ref):
    @pl.when(pl.program_id(1) == 0)
    def _():
        wb_ref[...] = w_ref[...].astype(BF16)

    lin = jnp.dot(a_ref[...], wb_ref[...], preferred_element_type=F32) + b_ref[...]
    gg = g_ref[...].astype(F32)
    o_ref[...] = (gg * jax.nn.sigmoid(lin) * _silu(z_ref[...].astype(F32))).astype(o_ref.dtype)


def _glu(gact, w_all, b_all, uz, j, tm=512, tn=1024):
    t, e = gact.shape
    zoff = e // tn
    return pl.pallas_call(
        _glu_kernel,
        grid=(e // tn, t // tm),
        in_specs=[pl.BlockSpec((tm, e), lambda c, r: (r, 0)),
                  pl.BlockSpec((None, e, tn), lambda c, r: (j, 0, c),
                               pipeline_mode=pl.Buffered(1)),
                  pl.BlockSpec((None, 1, tn), lambda c, r: (j, 0, c)),
                  pl.BlockSpec((tm, tn), lambda c, r: (r, c)),
                  pl.BlockSpec((tm, tn), lambda c, r: (r, c + zoff))],
        out_specs=pl.BlockSpec((tm, tn), lambda c, r: (r, c)),
        out_shape=jax.ShapeDtypeStruct((t, e), BF16),
        scratch_shapes=[pltpu.VMEM((e, tn), BF16)],
        compiler_params=_cparams(),
        name="glu",
    )(gact, w_all, b_all, gact, uz)


def _cast_kernel(w_ref, o_ref):
    o_ref[...] = w_ref[...].astype(o_ref.dtype)


def _cast_bf16(w, tr=1024):
    n, r, c = w.shape
    blk = pl.BlockSpec((None, tr, c), lambda a, i: (a, i, 0))
    return pl.pallas_call(
        _cast_kernel,
        grid=(n, r // tr),
        in_specs=[blk],
        out_specs=blk,
        out_shape=jax.ShapeDtypeStruct(w.shape, BF16),
        compiler_params=_cparams(),
        name="cast_bf16",
    )(w)


def _out_proj_kernel(y_ref, w_ref, x_ref, g_ref, *o_refs, last):
    xn = x_ref[...] + jnp.dot(y_ref[...], w_ref[...], preferred_element_type=F32)
    inv = lax.rsqrt(jnp.mean(xn * xn, axis=-1, keepdims=True) + NORM_EPS)
    hn = xn * inv * g_ref[...]
    if last:
        o_refs[0][...] = hn
    else:
        o_refs[0][...] = xn
        o_refs[1][...] = hn.astype(o_refs[1].dtype)


def _out_proj(y, w_all, j, x, gains, layer, last, tm=256):
    t, e = y.shape
    d = w_all.shape[2]
    row = pl.BlockSpec((tm, d), lambda i: (i, 0))
    if last:
        out_shape = jax.ShapeDtypeStruct((t, d), F32)
        out_specs = row
    else:
        out_shape = (jax.ShapeDtypeStruct((t, d), F32), jax.ShapeDtypeStruct((t, d), BF16))
        out_specs = (row, row)
    return pl.pallas_call(
        functools.partial(_out_proj_kernel, last=last),
        grid=(t // tm,),
        in_specs=[pl.BlockSpec((tm, e), lambda i: (i, 0)),
                  pl.BlockSpec((None, e, d), lambda i: (j, 0, 0)),
                  row,
                  pl.BlockSpec((None, 1, d), lambda i: (layer, 0, 0))],
        out_specs=out_specs,
        out_shape=out_shape,
        compiler_params=_cparams(),
        name="out_proj",
    )(y, w_all, x, gains)


def kernel(x, norm_g, final_norm_g, pool_w_in, pool_w_grp, pool_scale, pool_w_out, ssm_w_in, ssm_a_re, ssm_a_im, ssm_log_dt, ssm_b_re, ssm_b_im, ssm_c_re, ssm_c_im, ssm_d, ssm_w_glu, ssm_b_glu, ssm_w_out):
    nb, seq_len, d = x.shape
    depth = norm_g.shape[0]
    segs = LANES // nb
    ni = seq_len // (CHUNK * segs)
    assert nb * segs == LANES and ni * segs * CHUNK == seq_len
    t = nb * seq_len

    xs = x.reshape(nb, segs, ni, CHUNK, d).transpose(3, 2, 0, 1, 4).reshape(t, d)
    gains = jnp.concatenate([norm_g, final_norm_g[None]], axis=0)[:, None, :]
    pool_scale3 = pool_scale[:, None, :]
    ssm_b_glu3 = ssm_b_glu[:, None, :]
    pool_w_out_b = _cast_bf16(pool_w_out)
    ssm_w_out_b = _cast_bf16(ssm_w_out)

    h = _rmsnorm(xs, gains, 0)
    for layer in range(depth):
        j = layer // 2
        if layer % 2 == 0:
            uz = _in_proj(h, pool_w_in, j)
            y = _pool_mixer(uz, pool_w_grp, pool_scale3, j, nb, ni)
            w_out = pool_w_out_b
        else:
            uz = _in_proj(h, ssm_w_in, j)
            tab, coef, c1s, c2s = _ssm_tables(
                ssm_a_re[j], ssm_a_im[j], ssm_log_dt[j], ssm_b_re[j], ssm_b_im[j],
                ssm_c_re[j], ssm_c_im[j], ni * CHUNK, nb)
            gact = _ssm_mixer(uz, tab, coef, c1s, c2s, ssm_d[j], nb, ni)
            y = _glu(gact, ssm_w_glu, ssm_b_glu3, uz, j)
            w_out = ssm_w_out_b
        last = layer == depth - 1
        res = _out_proj(y, w_out, j, xs, gains, layer + 1, last)
        if last:
            return res.reshape(CHUNK, ni, nb, segs, d).transpose(2, 3, 1, 0, 4).reshape(nb, seq_len, d)
        xs, h = res
```

```python
import functools
import math

import jax
import jax.numpy as jnp
import numpy as np
from jax import lax
from jax.experimental import pallas as pl
from jax.experimental.pallas import tpu as pltpu

F32 = jnp.float32
BF16 = jnp.bfloat16

POOL_WINDOWS = (2, 4, 8, 16)
SSM_P = 16
SSM_N = 64
CHUNK = 16
LANES = 128
GLU_SUB = 256
PACK = 16
NORM_EPS = 1e-6
VMEM_LIMIT = 56 * 1024 * 1024

_T_F1, _T_F2, _T_G1, _T_G2, _T_D1, _T_D2, _T_POW_ROWS = 0, 16, 32, 48, 64, 88, 112
_T_BP, _T_BQ, _T_CR, _T_CI, _T_BC_ROWS = 0, 16, 32, 48, 64


def _cparams():
    return pltpu.CompilerParams(vmem_limit_bytes=VMEM_LIMIT)


def _silu(z):
    return z * jax.nn.sigmoid(z)


def _gelu_tanh(y):
    a = -2.0 * math.sqrt(2.0 / math.pi) * math.log2(math.e)
    return y / (1.0 + jnp.exp2(y * (a + (a * 0.044715) * (y * y))))


def _rmsnorm_kernel(x_ref, g_ref, o_ref):
    x = x_ref[...]
    inv = lax.rsqrt(jnp.mean(x * x, axis=-1, keepdims=True) + NORM_EPS)
    o_ref[...] = (x * inv * g_ref[...]).astype(o_ref.dtype)


def _rmsnorm(x, gains, layer, tm=512):
    t, d = x.shape
    return pl.pallas_call(
        _rmsnorm_kernel,
        grid=(t // tm,),
        in_specs=[pl.BlockSpec((tm, d), lambda i: (i, 0)),
                  pl.BlockSpec((None, 1, d), lambda i: (layer, 0, 0))],
        out_specs=pl.BlockSpec((tm, d), lambda i: (i, 0)),
        out_shape=jax.ShapeDtypeStruct((t, d), BF16),
        compiler_params=_cparams(),
        name="rmsnorm",
    )(x, gains)


def _in_proj_kernel(a_ref, w_ref, o_ref, wb_ref):
    @pl.when(pl.program_id(1) == 0)
    def _():
        wb_ref[...] = w_ref[...].astype(BF16)

    o_ref[...] = jnp.dot(a_ref[...], wb_ref[...],
                         preferred_element_type=F32).astype(o_ref.dtype)


def _in_proj(h, w_all, j, tm=1024, tn=1024):
    t, k = h.shape
    n = w_all.shape[2]
    return pl.pallas_call(
        _in_proj_kernel,
        grid=(n // tn, t // tm),
        in_specs=[pl.BlockSpec((tm, k), lambda c, r: (r, 0)),
                  pl.BlockSpec((None, k, tn), lambda c, r: (j, 0, c))],
        out_specs=pl.BlockSpec((tm, tn), lambda c, r: (r, c)),
        out_shape=jax.ShapeDtypeStruct((t, n), BF16),
        scratch_shapes=[pltpu.VMEM((k, tn), BF16)],
        compiler_params=_cparams(),
        name="in_proj",
    )(h, w_all)


def _pool_kernel(u_ref, prev_ref, z_ref, wg_ref, sc_ref, o_ref, wb_ref, hist_ref, p_ref,
                 *, rows, cg, sub):
    g = pl.program_id(0)
    i = pl.program_id(1)
    b = pl.program_id(2)

    @pl.when((i == 0) & (b == 0))
    def _():
        wb_ref[...] = wg_ref[...].astype(BF16)

    seg0 = lax.broadcasted_iota(jnp.int32, (rows, 1), 0) == 0
    n_r, n_c = rows // PACK, cg // LANES

    def windows(w, first_chunk):
        def strip(idx, carry):
            r0 = pl.multiple_of((idx // n_c) * PACK, PACK)
            c0 = pl.multiple_of((idx % n_c) * LANES, LANES)
            rs, cs = pl.ds(r0, PACK), pl.ds(c0, LANES)
            cur = [u_ref[t, rs, cs].astype(F32) for t in range(CHUNK)]
            s = {t: cur[t] for t in range(CHUNK)}
            for m in range(1, w):
                if first_chunk:
                    s[-m] = hist_ref[m - 1, rs, cs]
                else:
                    s[-m] = prev_ref[CHUNK - m, rs, cs].astype(F32)
            lo, k = -(w - 1), 1
            while k < w:
                s = {t: s[t] + s[t - k] for t in range(lo + k, CHUNK)}
                lo += k
                k *= 2
            if first_chunk:
                first = lax.broadcasted_iota(jnp.int32, (PACK, 1), 0) + r0 == 0
            for t in range(CHUNK):
                if first_chunk:
                    inv = jnp.where(first, 1.0 / min(t + 1, w), 1.0 / w)
                else:
                    inv = 1.0 / w
                p_ref[t, rs, cs] = (s[t] * inv - cur[t]).astype(BF16)
            return carry

        lax.fori_loop(0, n_r * n_c, strip, 0, unroll=2)

    for gi, w in enumerate(POOL_WINDOWS):
        @pl.when((g == gi) & (i == 0))
        def _(w=w):
            for m in range(1, w):
                blk = prev_ref[CHUNK - m].astype(F32)
                hist_ref[m - 1] = jnp.where(seg0, 0.0, pltpu.roll(blk, 1, axis=0))
            windows(w, True)

        @pl.when((g == gi) & (i != 0))
        def _(w=w):
            windows(w, False)

    for q in range(CHUNK // sub):
        ts = slice(q * sub, (q + 1) * sub)
        p = p_ref[ts].reshape(sub * rows, cg)
        m = jnp.dot(p, wb_ref[...], preferred_element_type=F32)
        zz = z_ref[ts].reshape(sub * rows, cg).astype(F32)
        y = m * sc_ref[...] * _silu(zz)
        o_ref[ts] = y.astype(o_ref.dtype).reshape(sub, rows, cg)


def _pool_mixer(uz, w_grp_all, scale_all, j, nb, ni):
    t, e2 = uz.shape
    e = e2 // 2
    ng, cg = w_grp_all.shape[1], w_grp_all.shape[2]
    nc = t // CHUNK
    rows = nc // (ni * nb)
    uz3 = uz.reshape(CHUNK, nc, e2)
    blk = (CHUNK, rows, cg)
    kern = functools.partial(_pool_kernel, rows=rows, cg=cg, sub=4)
    out = pl.pallas_call(
        kern,
        grid=(ng, ni, nb),
        in_specs=[pl.BlockSpec(blk, lambda g, i, b: (0, i * nb + b, g)),
                  pl.BlockSpec(blk, lambda g, i, b: (0, ((i + ni - 1) % ni) * nb + b, g)),
                  pl.BlockSpec(blk, lambda g, i, b: (0, i * nb + b, ng + g)),
                  pl.BlockSpec((None, None, cg, cg), lambda g, i, b: (j, g, 0, 0)),
                  pl.BlockSpec((None, 1, cg), lambda g, i, b: (j, 0, g))],
        out_specs=pl.BlockSpec(blk, lambda g, i, b: (0, i * nb + b, g)),
        out_shape=jax.ShapeDtypeStruct((CHUNK, nc, e), BF16),
        scratch_shapes=[pltpu.VMEM((cg, cg), BF16),
                        pltpu.VMEM((CHUNK - 1, rows, cg), F32),
                        pltpu.VMEM((CHUNK, rows, cg), BF16)],
        compiler_params=_cparams(),
        name="pool_mixer",
    )(uz3, uz3, uz3, w_grp_all, scale_all)
    return out.reshape(t, e)


def _causal_mask():
    n = CHUNK * SSM_P
    r_tau = lax.broadcasted_iota(jnp.int32, (n, n), 0) // SSM_P
    c_tau = lax.broadcasted_iota(jnp.int32, (n, n), 1) // SSM_P
    return jnp.where(r_tau >= c_tau, 1.0, 0.0).astype(BF16)


def _gen_group_weights(tab, tbc, mask):
    def row(r):
        return jnp.broadcast_to(tab[r:r + 1, :], (SSM_P, 2 * SSM_N))

    bp, bq = tbc[_T_BP:_T_BP + SSM_P], tbc[_T_BQ:_T_BQ + SSM_P]
    cr, ci = tbc[_T_CR:_T_CR + SSM_P], tbc[_T_CI:_T_CI + SSM_P]
    wb = jnp.concatenate([bp * row(_T_F1 + t) + bq * row(_T_F2 + t) for t in range(CHUNK)], axis=0)
    rn = jnp.concatenate([bp * row(_T_G1 + t) + bq * row(_T_G2 + t) for t in range(CHUNK)], axis=0)
    cb = [cr * row(_T_D1 + d) + ci * row(_T_D2 + d) for d in range(CHUNK + 1)]
    lm = jnp.concatenate(cb[:CHUNK], axis=0)
    wc = jnp.concatenate(cb[1:], axis=0)
    lm_h, rn_h = lm.astype(BF16), rn.astype(BF16)
    lm_l = (lm - lm_h.astype(F32)).astype(BF16)
    rn_l = (rn - rn_h.astype(F32)).astype(BF16)
    kt = lax.dot_general(jnp.concatenate([lm_h, lm_h, lm_l], axis=1),
                         jnp.concatenate([rn_h, rn_l, rn_h], axis=1),
                         (((1,), (1,)), ((), ())), preferred_element_type=F32)
    return wb.T.astype(BF16), wc.astype(BF16), kt.astype(BF16) * mask


def _ssm_kernel(u_ref, tab_ref, tbc_ref, co_ref, c1_ref, c2_ref, d_ref, o_ref, xt_ref, ot_ref, wc_ref,
                kt_ref, st_ref, a_ref, ep_ref, eq_ref, hp_ref, mask_ref, *, gb, nc, ni, segs):
    n = SSM_N
    kk = CHUNK * SSM_P
    gl = LANES

    @pl.when(pl.program_id(0) == 0)
    def _():
        mask_ref[...] = _causal_mask()

    for t in range(CHUNK):
        xt_ref[t] = u_ref[t].astype(F32).T.astype(BF16)

    def cmul(ar, ai, xr, xi):
        return ar * xr - ai * xi, ar * xi + ai * xr

    def group_rows(j):
        return pl.ds(pl.multiple_of(j * SSM_P, SSM_P), SSM_P)

    def scan_rows(j):
        return pl.ds(pl.multiple_of(j * gl, gl), gl)

    def tiles(st):
        return ([st[:n, i * LANES:(i + 1) * LANES] for i in range(ni)],
                [st[n:, i * LANES:(i + 1) * LANES] for i in range(ni)])

    def head(j, carry):
        ut = xt_ref[:, group_rows(j), :].reshape(kk, nc)
        wbt, wct, ktt = _gen_group_weights(tab_ref[j], tbc_ref[j], mask_ref[...])
        wc_ref[j] = wct
        kt_ref[j] = ktt
        st = jnp.dot(wbt, ut, preferred_element_type=F32)
        st_ref[j] = st
        s_r, s_i = tiles(st)
        co = co_ref[j]
        ar = jnp.broadcast_to(co[:, 0:1], (n, LANES))
        ai = jnp.broadcast_to(co[:, 1:2], (n, LANES))
        a_ref[j] = jnp.concatenate([ar, ai], axis=0)
        er, ei = s_r[0], s_i[0]
        for i in range(1, ni):
            mr, mi = cmul(ar, ai, er, ei)
            er, ei = mr + s_r[i], mi + s_i[i]
        ep_ref[scan_rows(j), :] = jnp.concatenate([er, ei], axis=0).T
        eq_ref[scan_rows(j), :] = jnp.concatenate([ei, er], axis=0).T
        return carry

    lax.fori_loop(0, gb, head, 0, unroll=2)

    c1, c2 = c1_ref[...], c2_ref[...]
    pairs = gb * (gl // segs)

    def seg_step(s, carry):
        p, q = carry
        at = pl.ds(s, pairs, stride=segs)
        hp_ref[at, :] = p
        return (c1 * p + c2 * q + ep_ref[at, :], c1 * q - c2 * p + eq_ref[at, :])

    zero = jnp.zeros((pairs, 2 * n), F32)
    lax.fori_loop(0, segs, seg_step, (zero, zero), unroll=8)

    def tail(j, carry):
        rows = group_rows(j)
        ut = xt_ref[:, rows, :].reshape(kk, nc)
        s_r, s_i = tiles(st_ref[j])
        a = a_ref[j]
        ar, ai = a[:n], a[n:]
        e = hp_ref[scan_rows(j), :].T
        hr, hi = e[:n], e[n:]
        h_tiles = []
        for i in range(ni):
            h_tiles.append(jnp.concatenate([hr, hi], axis=0).astype(BF16))
            if i + 1 < ni:
                mr, mi = cmul(ar, ai, hr, hi)
                hr, hi = mr + s_r[i], mi + s_i[i]
        ht = jnp.concatenate(h_tiles, axis=1)
        yt = jnp.dot(kt_ref[j], ut, preferred_element_type=F32)
        yt = yt + jnp.dot(wc_ref[j], ht, preferred_element_type=F32)
        d_rows = jnp.concatenate([d_ref[j]] * CHUNK, axis=0)
        yt = yt + jnp.concatenate([d_rows] * ni, axis=1) * ut.astype(F32)
        ot_ref[:, rows, :] = _gelu_tanh(yt).reshape(CHUNK, SSM_P, nc)
        return carry

    lax.fori_loop(0, gb, tail, 0, unroll=2)

    for t in range(CHUNK):
        o_ref[t] = ot_ref[t].T.astype(o_ref.dtype)


def _ssm_mixer(uz, tables, j, nb, ni, gb=8):
    tab, tbc, coef, c1s, c2s, d_tab = tables
    t, e2 = uz.shape
    e = e2 // 2
    g = e // SSM_P
    nc = t // CHUNK
    segs = LANES // nb
    uz3 = uz.reshape(CHUNK, nc, e2)
    wl = gb * SSM_P
    kern = functools.partial(_ssm_kernel, gb=gb, nc=nc, ni=ni, segs=segs)
    out = pl.pallas_call(
        kern,
        grid=(g // gb,),
        in_specs=[pl.BlockSpec((CHUNK, nc, wl), lambda i: (0, 0, i)),
                  pl.BlockSpec((None, gb, _T_POW_ROWS, 2 * SSM_N), lambda i: (j, i, 0, 0)),
                  pl.BlockSpec((None, gb, _T_BC_ROWS, 2 * SSM_N), lambda i: (j, i, 0, 0)),
                  pl.BlockSpec((None, gb, SSM_N, 2), lambda i: (j, i, 0, 0)),
                  pl.BlockSpec((None, gb * nb, 2 * SSM_N), lambda i: (j, i, 0)),
                  pl.BlockSpec((None, gb * nb, 2 * SSM_N), lambda i: (j, i, 0)),
                  pl.BlockSpec((None, gb, SSM_P, LANES), lambda i: (j, i, 0, 0))],
        out_specs=pl.BlockSpec((CHUNK, nc, wl), lambda i: (0, 0, i)),
        out_shape=jax.ShapeDtypeStruct((CHUNK, nc, e), BF16),
        scratch_shapes=[pltpu.VMEM((CHUNK, wl, nc), BF16),
                        pltpu.VMEM((CHUNK, wl, nc), F32),
                        pltpu.VMEM((gb, CHUNK * SSM_P, 2 * SSM_N), BF16),
                        pltpu.VMEM((gb, CHUNK * SSM_P, CHUNK * SSM_P), BF16),
                        pltpu.VMEM((gb, 2 * SSM_N, nc), F32),
                        pltpu.VMEM((gb, 2 * SSM_N, LANES), F32),
                        pltpu.VMEM((gb * LANES, 2 * SSM_N), F32),
                        pltpu.VMEM((gb * LANES, 2 * SSM_N), F32),
                        pltpu.VMEM((gb * LANES, 2 * SSM_N), F32),
                        pltpu.VMEM((CHUNK * SSM_P, CHUNK * SSM_P), BF16)],
        compiler_params=_cparams(),
        name="ssm_mixer",
    )(uz3, tab, tbc, coef, c1s, c2s, d_tab)
    return out.reshape(t, e)


def _pow_table_plan():
    n = SSM_N
    k = np.zeros((_T_POW_ROWS,), np.float32)
    wr = np.zeros((_T_POW_ROWS, 2 * n), np.float32)
    wi = np.zeros((_T_POW_ROWS, 2 * n), np.float32)
    for tau in range(CHUNK):
        k[_T_F1 + tau] = k[_T_F2 + tau] = CHUNK - 1 - tau
        k[_T_G1 + tau] = k[_T_G2 + tau] = -tau
        for base in (_T_F1, _T_G1):
            wr[base + tau, :] = 1.0
        for base in (_T_F2, _T_G2):
            wi[base + tau, :n], wi[base + tau, n:] = -1.0, 1.0
    for d in range(CHUNK + 1):
        k[_T_D1 + d] = k[_T_D2 + d] = d
        wr[_T_D1 + d, :n], wi[_T_D1 + d, n:] = 1.0, -1.0
        wi[_T_D2 + d, :n], wr[_T_D2 + d, n:] = -1.0, -1.0
    return k, wr, wi


def _ssm_tables(a_re, a_im, log_dt, b_re, b_im, c_re, c_im, d_skip, seg_tokens, nb):
    dt = jnp.exp(log_dt)[..., None]
    la, th = a_re * dt, a_im * dt
    mag = jnp.exp(la)
    abr = mag * jnp.cos(th)
    abi = mag * jnp.sin(th)
    den = a_re * a_re + a_im * a_im
    nr = abr - 1.0
    fr = (nr * a_re + abi * a_im) / den
    fi = (abi * a_re - nr * a_im) / den
    bbr = fr[..., None] * b_re - fi[..., None] * b_im
    bbi = fr[..., None] * b_im + fi[..., None] * b_re

    def cat(x, y):
        return jnp.concatenate([x, y], axis=-1)

    def power(k, la_, th_):
        m = jnp.exp(k * la_)
        return m * jnp.cos(k * th_), m * jnp.sin(k * th_)

    k, wr, wi = _pow_table_plan()
    pr, pi = power(k[:, None], cat(la, la)[:, :, None, :], cat(th, th)[:, :, None, :])
    tab = wr * pr + wi * pi

    bt_r, bt_i = jnp.swapaxes(bbr, -1, -2), jnp.swapaxes(bbi, -1, -2)
    tbc = jnp.concatenate([cat(bt_r, bt_i), cat(bt_i, bt_r),
                           cat(c_re, c_re), cat(c_im, c_im)], axis=2)

    coef = jnp.stack(power(float(CHUNK), la, th), axis=-1)
    sr, si = power(float(seg_tokens), la, th)
    c1s = jnp.repeat(cat(sr, sr), nb, axis=1)
    c2s = jnp.repeat(cat(-si, si), nb, axis=1)
    d_tab = jnp.broadcast_to(d_skip.reshape(d_skip.shape[0], -1, SSM_P, 1),
                             (d_skip.shape[0], d_skip.shape[1] // SSM_P, SSM_P, LANES))
    return tab, tbc, coef, c1s, c2s, d_tab


def _glu_kernel(a_ref, w_ref, b_ref, g_ref, z_ref, o_ref, wb_ref):
    @pl.when(pl.program_id(1) == 0)
    def _():
        wb_ref[...] = w_ref[...].astype(BF16)

    for s in range(o_ref.shape[1] // GLU_SUB):
        cs = slice(s * GLU_SUB, (s + 1) * GLU_SUB)
        lin = jnp.dot(a_ref[...], wb_ref[:, cs], preferred_element_type=F32) + b_ref[:, cs]
        gg = g_ref[:, cs].astype(F32)
        gate = jax.nn.sigmoid(lin) * _silu(z_ref[:, cs].astype(F32))
        o_ref[:, cs] = (gg * gate).astype(o_ref.dtype)


def _glu(gact, w_all, b_all, uz, j, tm=512, tn=1024):
    t, e = gact.shape
    zoff = e // tn
    return pl.pallas_call(
        _glu_kernel,
        grid=(e // tn, t // tm),
        in_specs=[pl.BlockSpec((tm, e), lambda c, r: (r, 0)),
                  pl.BlockSpec((None, e, tn), lambda c, r: (j, 0, c),
                               pipeline_mode=pl.Buffered(1)),
                  pl.BlockSpec((None, 1, tn), lambda c, r: (j, 0, c)),
                  pl.BlockSpec((tm, tn), lambda c, r: (r, c)),
                  pl.BlockSpec((tm, tn), lambda c, r: (r, c + zoff))],
        out_specs=pl.BlockSpec((tm, tn), lambda c, r: (r, c)),
        out_shape=jax.ShapeDtypeStruct((t, e), BF16),
        scratch_shapes=[pltpu.VMEM((e, tn), BF16)],
        compiler_params=_cparams(),
        name="glu",
    )(gact, w_all, b_all, gact, uz)


def _cast_kernel(w_ref, o_ref):
    o_ref[...] = w_ref[...].astype(o_ref.dtype)


def _cast_bf16(w, tr=1024):
    n, r, c = w.shape
    blk = pl.BlockSpec((None, tr, c), lambda a, i: (a, i, 0))
    return pl.pallas_call(
        _cast_kernel,
        grid=(n, r // tr),
        in_specs=[blk],
        out_specs=blk,
        out_shape=jax.ShapeDtypeStruct(w.shape, BF16),
        compiler_params=_cparams(),
        name="cast_bf16",
    )(w)


def _out_proj_kernel(y_ref, w_ref, x_ref, g_ref, *o_refs, last):
    xn = x_ref[...] + jnp.dot(y_ref[...], w_ref[...], preferred_element_type=F32)
    inv = lax.rsqrt(jnp.mean(xn * xn, axis=-1, keepdims=True) + NORM_EPS)
    hn = xn * inv * g_ref[...]
    if last:
        o_refs[0][...] = hn
    else:
        o_refs[0][...] = xn
        o_refs[1][...] = hn.astype(o_refs[1].dtype)


def _out_proj(y, w_all, j, x, gains, layer, last, tm=256):
    t, e = y.shape
    d = w_all.shape[2]
    row = pl.BlockSpec((tm, d), lambda i: (i, 0))
    if last:
        out_shape = jax.ShapeDtypeStruct((t, d), F32)
        out_specs = row
    else:
        out_shape = (jax.ShapeDtypeStruct((t, d), F32), jax.ShapeDtypeStruct((t, d), BF16))
        out_specs = (row, row)
    return pl.pallas_call(
        functools.partial(_out_proj_kernel, last=last),
        grid=(t // tm,),
        in_specs=[pl.BlockSpec((tm, e), lambda i: (i, 0)),
                  pl.BlockSpec((None, e, d), lambda i: (j, 0, 0)),
                  row,
                  pl.BlockSpec((None, 1, d), lambda i: (layer, 0, 0))],
        out_specs=out_specs,
        out_shape=out_shape,
        compiler_params=_cparams(),
        name="out_proj",
    )(y, w_all, x, gains)


def kernel(x, norm_g, final_norm_g, pool_w_in, pool_w_grp, pool_scale, pool_w_out, ssm_w_in, ssm_a_re, ssm_a_im, ssm_log_dt, ssm_b_re, ssm_b_im, ssm_c_re, ssm_c_im, ssm_d, ssm_w_glu, ssm_b_glu, ssm_w_out):
    nb, seq_len, d = x.shape
    depth = norm_g.shape[0]
    segs = LANES // nb
    ni = seq_len // (CHUNK * segs)
    assert nb * segs == LANES and ni * segs * CHUNK == seq_len
    t = nb * seq_len

    xs = x.reshape(nb, segs, ni, CHUNK, d).transpose(3, 2, 0, 1, 4).reshape(t, d)
    gains = jnp.concatenate([norm_g, final_norm_g[None]], axis=0)[:, None, :]
    pool_scale3 = pool_scale[:, None, :]
    ssm_b_glu3 = ssm_b_glu[:, None, :]
    pool_w_out_b = _cast_bf16(pool_w_out)
    ssm_w_out_b = _cast_bf16(ssm_w_out)

    ssm_tables = _ssm_tables(ssm_a_re, ssm_a_im, ssm_log_dt, ssm_b_re, ssm_b_im,
                             ssm_c_re, ssm_c_im, ssm_d, ni * CHUNK, nb)

    h = _rmsnorm(xs, gains, 0)
    for layer in range(depth):
        j = layer // 2
        if layer % 2 == 0:
            uz = _in_proj(h, pool_w_in, j)
            y = _pool_mixer(uz, pool_w_grp, pool_scale3, j, nb, ni)
            w_out = pool_w_out_b
        else:
            uz = _in_proj(h, ssm_w_in, j)
            gact = _ssm_mixer(uz, ssm_tables, j, nb, ni)
            y = _glu(gact, ssm_w_glu, ssm_b_glu3, uz, j)
            w_out = ssm_w_out_b
        last = layer == depth - 1
        res = _out_proj(y, w_out, j, xs, gains, layer + 1, last)
        if last:
            return res.reshape(CHUNK, ni, nb, segs, d).transpose(2, 3, 1, 0, 4).reshape(nb, seq_len, d)
        xs, h = res
```

```python
import functools
import math

import jax
import jax.numpy as jnp
import numpy as np
from jax import lax
from jax.experimental import pallas as pl
from jax.experimental.pallas import tpu as pltpu

F32 = jnp.float32
BF16 = jnp.bfloat16

POOL_WINDOWS = (2, 4, 8, 16)
SSM_P = 16
SSM_N = 64
CHUNK = 16
LANES = 128
GLU_SUB = 256
PACK = 16
LOG2E = math.log2(math.e)
NORM_EPS = 1e-6
VMEM_LIMIT = 56 * 1024 * 1024

_T_POW_ROWS = 4 * CHUNK
_T_BP, _T_BQ, _T_CR, _T_CI, _T_BC_ROWS = 0, 16, 32, 48, 64


def _cparams():
    return pltpu.CompilerParams(vmem_limit_bytes=VMEM_LIMIT)


def _silu(z):
    return z * jax.nn.sigmoid(z)


def _gelu_tanh(y):
    a = -2.0 * math.sqrt(2.0 / math.pi) * math.log2(math.e)
    return y / (1.0 + jnp.exp2(y * (a + (a * 0.044715) * (y * y))))


def _rmsnorm_kernel(x_ref, g_ref, o_ref):
    x = x_ref[...]
    inv = lax.rsqrt(jnp.mean(x * x, axis=-1, keepdims=True) + NORM_EPS)
    o_ref[...] = (x * inv * g_ref[...]).astype(o_ref.dtype)


def _rmsnorm(x, gains, layer, tm=512):
    t, d = x.shape
    return pl.pallas_call(
        _rmsnorm_kernel,
        grid=(t // tm,),
        in_specs=[pl.BlockSpec((tm, d), lambda i: (i, 0)),
                  pl.BlockSpec((None, 1, d), lambda i: (layer, 0, 0))],
        out_specs=pl.BlockSpec((tm, d), lambda i: (i, 0)),
        out_shape=jax.ShapeDtypeStruct((t, d), BF16),
        compiler_params=_cparams(),
        name="rmsnorm",
    )(x, gains)


def _in_proj_kernel(a_ref, w_ref, o_ref, wb_ref):
    @pl.when(pl.program_id(1) == 0)
    def _():
        wb_ref[...] = w_ref[...].astype(BF16)

    o_ref[...] = jnp.dot(a_ref[...], wb_ref[...],
                         preferred_element_type=F32).astype(o_ref.dtype)


def _in_proj(h, w_all, j, tm=1024, tn=1024):
    t, k = h.shape
    n = w_all.shape[2]
    return pl.pallas_call(
        _in_proj_kernel,
        grid=(n // tn, t // tm),
        in_specs=[pl.BlockSpec((tm, k), lambda c, r: (r, 0)),
                  pl.BlockSpec((None, k, tn), lambda c, r: (j, 0, c))],
        out_specs=pl.BlockSpec((tm, tn), lambda c, r: (r, c)),
        out_shape=jax.ShapeDtypeStruct((t, n), BF16),
        scratch_shapes=[pltpu.VMEM((k, tn), BF16)],
        compiler_params=_cparams(),
        name="in_proj",
    )(h, w_all)


def _pool_kernel(u_ref, prev_ref, z_ref, wg_ref, sc_ref, o_ref, wb_ref, hist_ref, p_ref,
                 *, rows, cg, sub):
    g = pl.program_id(0)
    i = pl.program_id(1)
    b = pl.program_id(2)

    @pl.when((i == 0) & (b == 0))
    def _():
        wb_ref[...] = wg_ref[...].astype(BF16)

    seg0 = lax.broadcasted_iota(jnp.int32, (rows, 1), 0) == 0
    n_r, n_c = rows // PACK, cg // LANES

    def windows(w, first_chunk):
        def strip(idx, carry):
            r0 = pl.multiple_of((idx // n_c) * PACK, PACK)
            c0 = pl.multiple_of((idx % n_c) * LANES, LANES)
            rs, cs = pl.ds(r0, PACK), pl.ds(c0, LANES)
            cur = [u_ref[t, rs, cs].astype(F32) for t in range(CHUNK)]
            s = {t: cur[t] for t in range(CHUNK)}
            for m in range(1, w):
                if first_chunk:
                    s[-m] = hist_ref[m - 1, rs, cs]
                else:
                    s[-m] = prev_ref[CHUNK - m, rs, cs].astype(F32)
            lo, k = -(w - 1), 1
            while k < w:
                s = {t: s[t] + s[t - k] for t in range(lo + k, CHUNK)}
                lo += k
                k *= 2
            if first_chunk:
                first = lax.broadcasted_iota(jnp.int32, (PACK, 1), 0) + r0 == 0
            for t in range(CHUNK):
                if first_chunk:
                    inv = jnp.where(first, 1.0 / min(t + 1, w), 1.0 / w)
                else:
                    inv = 1.0 / w
                p_ref[t, rs, cs] = (s[t] * inv - cur[t]).astype(BF16)
            return carry

        lax.fori_loop(0, n_r * n_c, strip, 0, unroll=2)

    for gi, w in enumerate(POOL_WINDOWS):
        @pl.when((g == gi) & (i == 0))
        def _(w=w):
            for m in range(1, w):
                blk = prev_ref[CHUNK - m].astype(F32)
                hist_ref[m - 1] = jnp.where(seg0, 0.0, pltpu.roll(blk, 1, axis=0))
            windows(w, True)

        @pl.when((g == gi) & (i != 0))
        def _(w=w):
            windows(w, False)

    for q in range(CHUNK // sub):
        ts = slice(q * sub, (q + 1) * sub)
        p = p_ref[ts].reshape(sub * rows, cg)
        m = jnp.dot(p, wb_ref[...], preferred_element_type=F32)
        zz = z_ref[ts].reshape(sub * rows, cg).astype(F32)
        y = m * sc_ref[...] * _silu(zz)
        o_ref[ts] = y.astype(o_ref.dtype).reshape(sub, rows, cg)


def _pool_mixer(uz, w_grp_all, scale_all, j, nb, ni):
    t, e2 = uz.shape
    e = e2 // 2
    ng, cg = w_grp_all.shape[1], w_grp_all.shape[2]
    nc = t // CHUNK
    rows = nc // (ni * nb)
    uz3 = uz.reshape(CHUNK, nc, e2)
    blk = (CHUNK, rows, cg)
    kern = functools.partial(_pool_kernel, rows=rows, cg=cg, sub=4)
    out = pl.pallas_call(
        kern,
        grid=(ng, ni, nb),
        in_specs=[pl.BlockSpec(blk, lambda g, i, b: (0, i * nb + b, g)),
                  pl.BlockSpec(blk, lambda g, i, b: (0, ((i + ni - 1) % ni) * nb + b, g)),
                  pl.BlockSpec(blk, lambda g, i, b: (0, i * nb + b, ng + g)),
                  pl.BlockSpec((None, None, cg, cg), lambda g, i, b: (j, g, 0, 0)),
                  pl.BlockSpec((None, 1, cg), lambda g, i, b: (j, 0, g))],
        out_specs=pl.BlockSpec(blk, lambda g, i, b: (0, i * nb + b, g)),
        out_shape=jax.ShapeDtypeStruct((CHUNK, nc, e), BF16),
        scratch_shapes=[pltpu.VMEM((cg, cg), BF16),
                        pltpu.VMEM((CHUNK - 1, rows, cg), F32),
                        pltpu.VMEM((CHUNK, rows, cg), BF16)],
        compiler_params=_cparams(),
        name="pool_mixer",
    )(uz3, uz3, uz3, w_grp_all, scale_all)
    return out.reshape(t, e)


def _causal_mask():
    n = CHUNK * SSM_P
    r_tau = lax.broadcasted_iota(jnp.int32, (n, n), 0) // SSM_P
    c_tau = lax.broadcasted_iota(jnp.int32, (n, n), 1) // SSM_P
    return jnp.where(r_tau >= c_tau, 1.0, 0.0).astype(BF16)


def _gen_group_weights(tab, tbc, mask):
    def re(k):
        return jnp.broadcast_to(tab[CHUNK - k:CHUNK - k + 1, :], (SSM_P, 2 * SSM_N))

    def im(k):
        return jnp.broadcast_to(tab[3 * CHUNK - k:3 * CHUNK - k + 1, :], (SSM_P, 2 * SSM_N))

    low = lax.broadcasted_iota(jnp.int32, (SSM_P, 2 * SSM_N), 1) < SSM_N
    bp, bq = tbc[_T_BP:_T_BP + SSM_P], tbc[_T_BQ:_T_BQ + SSM_P]
    cr, ci = tbc[_T_CR:_T_CR + SSM_P], tbc[_T_CI:_T_CI + SSM_P]
    bqs = jnp.where(low, -bq, bq)
    ca = jnp.where(low, cr, -ci)
    cb_ = jnp.where(low, -ci, -cr)
    wb = jnp.concatenate([bp * re(CHUNK - 1 - t) + bqs * im(CHUNK - 1 - t) for t in range(CHUNK)], axis=0)
    rn = jnp.concatenate([bp * re(-t) + bqs * im(-t) for t in range(CHUNK)], axis=0)
    cb = [ca * re(d) + cb_ * im(d) for d in range(CHUNK + 1)]
    lm = jnp.concatenate(cb[:CHUNK], axis=0)
    wc = jnp.concatenate(cb[1:], axis=0)
    lm_h, rn_h = lm.astype(BF16), rn.astype(BF16)
    lm_l = (lm - lm_h.astype(F32)).astype(BF16)
    rn_l = (rn - rn_h.astype(F32)).astype(BF16)
    kt = lax.dot_general(jnp.concatenate([lm_h, lm_h, lm_l], axis=1),
                         jnp.concatenate([rn_h, rn_l, rn_h], axis=1),
                         (((1,), (1,)), ((), ())), preferred_element_type=F32)
    return wb.T.astype(BF16), wc.astype(BF16), kt.astype(BF16) * mask


def _ssm_kernel(u_ref, tab_ref, tbc_ref, co_ref, c1_ref, c2_ref, d_ref, o_ref, xt_ref, ot_ref,
                kw_ref, st_ref, a_ref, ep_ref, eq_ref, hp_ref, mask_ref, *, gb, nc, ni, segs):
    n = SSM_N
    kk = CHUNK * SSM_P
    gl = LANES

    @pl.when(pl.program_id(0) == 0)
    def _():
        mask_ref[...] = _causal_mask()

    for t in range(CHUNK):
        xt_ref[t] = u_ref[t].T

    def cmul(ar, ai, xr, xi):
        return ar * xr - ai * xi, ar * xi + ai * xr

    def group_rows(j):
        return pl.ds(pl.multiple_of(j * SSM_P, SSM_P), SSM_P)

    def scan_rows(j):
        return pl.ds(pl.multiple_of(j * gl, gl), gl)

    def tiles(st):
        return ([st[:n, i * LANES:(i + 1) * LANES] for i in range(ni)],
                [st[n:, i * LANES:(i + 1) * LANES] for i in range(ni)])

    def head(j, carry):
        ut = xt_ref[:, group_rows(j), :].reshape(kk, nc)
        wbt, wct, ktt = _gen_group_weights(tab_ref[j], tbc_ref[j], mask_ref[...])
        kw_ref[j, :, :kk] = ktt
        kw_ref[j, :, kk:] = wct
        st = jnp.dot(wbt, ut, preferred_element_type=F32)
        st_ref[j] = st
        s_r, s_i = tiles(st)
        co = co_ref[j]
        ar = jnp.broadcast_to(co[:, 0:1], (n, LANES))
        ai = jnp.broadcast_to(co[:, 1:2], (n, LANES))
        a_ref[j] = jnp.concatenate([ar, ai], axis=0)
        er, ei = s_r[0], s_i[0]
        for i in range(1, ni):
            mr, mi = cmul(ar, ai, er, ei)
            er, ei = mr + s_r[i], mi + s_i[i]
        ep_ref[scan_rows(j), :] = jnp.concatenate([er, ei], axis=0).T
        eq_ref[scan_rows(j), :] = jnp.concatenate([ei, er], axis=0).T
        return carry

    lax.fori_loop(0, gb, head, 0, unroll=4)

    c1, c2 = c1_ref[...], c2_ref[...]
    pairs = gb * (gl // segs)

    def seg_step(s, carry):
        p, q = carry
        at = pl.ds(s, pairs, stride=segs)
        hp_ref[at, :] = p
        return (c1 * p + c2 * q + ep_ref[at, :], c1 * q - c2 * p + eq_ref[at, :])

    zero = jnp.zeros((pairs, 2 * n), F32)
    lax.fori_loop(0, segs, seg_step, (zero, zero), unroll=8)

    def tail(j, carry):
        rows = group_rows(j)
        ut = xt_ref[:, rows, :].reshape(kk, nc)
        s_r, s_i = tiles(st_ref[j])
        a = a_ref[j]
        ar, ai = a[:n], a[n:]
        e = hp_ref[scan_rows(j), :].T
        hr, hi = e[:n], e[n:]
        h_tiles = []
        for i in range(ni):
            h_tiles.append(jnp.concatenate([hr, hi], axis=0).astype(BF16))
            if i + 1 < ni:
                mr, mi = cmul(ar, ai, hr, hi)
                hr, hi = mr + s_r[i], mi + s_i[i]
        ht = jnp.concatenate(h_tiles, axis=1)
        yt = jnp.dot(kw_ref[j], jnp.concatenate([ut, ht], axis=0), preferred_element_type=F32)
        skip = jnp.concatenate([d_ref[j]] * ni, axis=1)
        y3 = yt.reshape(CHUNK, SSM_P, nc) + skip[None] * ut.astype(F32).reshape(CHUNK, SSM_P, nc)
        ot_ref[:, rows, :] = _gelu_tanh(y3)
        return carry

    lax.fori_loop(0, gb, tail, 0, unroll=4)

    for t in range(CHUNK):
        o_ref[t] = ot_ref[t].T.astype(o_ref.dtype)


def _ssm_mixer(uz, tables, j, nb, ni, gb=8):
    tab, tbc, coef, c1s, c2s, d_tab = tables
    t, e2 = uz.shape
    e = e2 // 2
    g = e // SSM_P
    nc = t // CHUNK
    segs = LANES // nb
    uz3 = uz.reshape(CHUNK, nc, e2)
    wl = gb * SSM_P
    kern = functools.partial(_ssm_kernel, gb=gb, nc=nc, ni=ni, segs=segs)
    out = pl.pallas_call(
        kern,
        grid=(g // gb,),
        in_specs=[pl.BlockSpec((CHUNK, nc, wl), lambda i: (0, 0, i)),
                  pl.BlockSpec((None, gb, _T_POW_ROWS, 2 * SSM_N), lambda i: (j, i, 0, 0)),
                  pl.BlockSpec((None, gb, _T_BC_ROWS, 2 * SSM_N), lambda i: (j, i, 0, 0)),
                  pl.BlockSpec((None, gb, SSM_N, 2), lambda i: (j, i, 0, 0)),
                  pl.BlockSpec((None, gb * nb, 2 * SSM_N), lambda i: (j, i, 0)),
                  pl.BlockSpec((None, gb * nb, 2 * SSM_N), lambda i: (j, i, 0)),
                  pl.BlockSpec((None, gb, SSM_P, LANES), lambda i: (j, i, 0, 0))],
        out_specs=pl.BlockSpec((CHUNK, nc, wl), lambda i: (0, 0, i)),
        out_shape=jax.ShapeDtypeStruct((CHUNK, nc, e), BF16),
        scratch_shapes=[pltpu.VMEM((CHUNK, wl, nc), BF16),
                        pltpu.VMEM((CHUNK, wl, nc), F32),
                        pltpu.VMEM((gb, CHUNK * SSM_P, CHUNK * SSM_P + 2 * SSM_N), BF16),
                        pltpu.VMEM((gb, 2 * SSM_N, nc), F32),
                        pltpu.VMEM((gb, 2 * SSM_N, LANES), F32),
                        pltpu.VMEM((gb * LANES, 2 * SSM_N), F32),
                        pltpu.VMEM((gb * LANES, 2 * SSM_N), F32),
                        pltpu.VMEM((gb * LANES, 2 * SSM_N), F32),
                        pltpu.VMEM((CHUNK * SSM_P, CHUNK * SSM_P), BF16)],
        compiler_params=_cparams(),
        name="ssm_mixer",
    )(uz3, tab, tbc, coef, c1s, c2s, d_tab)
    return out.reshape(t, e)


def _ssm_tables(a_re, a_im, log_dt, b_re, b_im, c_re, c_im, d_skip, seg_tokens, nb):
    dt = jnp.exp(log_dt)[..., None]
    la, th = a_re * dt, a_im * dt
    mag = jnp.exp(la)
    abr = mag * jnp.cos(th)
    abi = mag * jnp.sin(th)
    den = a_re * a_re + a_im * a_im
    nr = abr - 1.0
    fr = (nr * a_re + abi * a_im) / den
    fi = (abi * a_re - nr * a_im) / den
    bbr = fr[..., None] * b_re - fi[..., None] * b_im
    bbi = fr[..., None] * b_im + fi[..., None] * b_re

    def cat(x, y):
        return jnp.concatenate([x, y], axis=-1)

    def power(k, la_, th_):
        m = jnp.exp(k * la_)
        return m * jnp.cos(k * th_), m * jnp.sin(k * th_)

    k = (CHUNK - np.arange(2 * CHUNK)).astype(np.float32)
    pr, pi = power(k[:, None], cat(la, la)[:, :, None, :], cat(th, th)[:, :, None, :])
    tab = jnp.concatenate([pr, pi], axis=2)

    bt_r, bt_i = jnp.swapaxes(bbr, -1, -2), jnp.swapaxes(bbi, -1, -2)
    tbc = jnp.concatenate([cat(bt_r, bt_i), cat(bt_i, bt_r),
                           cat(c_re, c_re), cat(c_im, c_im)], axis=2)

    coef = jnp.stack(power(float(CHUNK), la, th), axis=-1)
    sr, si = power(float(seg_tokens), la, th)
    c1s = jnp.repeat(cat(sr, sr), nb, axis=1)
    c2s = jnp.repeat(cat(-si, si), nb, axis=1)
    d_tab = jnp.broadcast_to(d_skip.reshape(d_skip.shape[0], -1, SSM_P, 1),
                             (d_skip.shape[0], d_skip.shape[1] // SSM_P, SSM_P, LANES))
    return tab, tbc, coef, c1s, c2s, d_tab


def _glu_kernel(a_ref, w_ref, b_ref, g_ref, z_ref, o_ref, wb_ref):
    @pl.when(pl.program_id(1) == 0)
    def _():
        wb_ref[...] = w_ref[...].astype(BF16)

    nsub = o_ref.shape[1] // GLU_SUB

    def matmul(s):
        return jnp.dot(a_ref[...], wb_ref[:, s * GLU_SUB:(s + 1) * GLU_SUB],
                       preferred_element_type=F32)

    acc = matmul(0)
    for s in range(nsub):
        cs = slice(s * GLU_SUB, (s + 1) * GLU_SUB)
        nxt = matmul(s + 1) if s + 1 < nsub else None
        gg = g_ref[:, cs].astype(F32)
        zz = z_ref[:, cs].astype(F32)
        den = (1.0 + jnp.exp2(-LOG2E * (acc + b_ref[:, cs]))) * (1.0 + jnp.exp2(-LOG2E * zz))
        o_ref[:, cs] = (gg * zz / den).astype(o_ref.dtype)
        acc = nxt


def _glu(gact, w_all, b_all, uz, j, tm=512, tn=1024):
    t, e = gact.shape
    zoff = e // tn
    return pl.pallas_call(
        _glu_kernel,
        grid=(e // tn, t // tm),
        in_specs=[pl.BlockSpec((tm, e), lambda c, r: (r, 0)),
                  pl.BlockSpec((None, e, tn), lambda c, r: (j, 0, c),
                               pipeline_mode=pl.Buffered(1)),
                  pl.BlockSpec((None, 1, tn), lambda c, r: (j, 0, c)),
                  pl.BlockSpec((tm, tn), lambda c, r: (r, c)),
                  pl.BlockSpec((tm, tn), lambda c, r: (r, c + zoff))],
        out_specs=pl.BlockSpec((tm, tn), lambda c, r: (r, c)),
        out_shape=jax.ShapeDtypeStruct((t, e), BF16),
        scratch_shapes=[pltpu.VMEM((e, tn), BF16)],
        compiler_params=_cparams(),
        name="glu",
    )(gact, w_all, b_all, gact, uz)


def _cast_kernel(w_ref, o_ref):
    o_ref[...] = w_ref[...].astype(o_ref.dtype)


def _cast_bf16(w, tr=1024):
    n, r, c = w.shape
    blk = pl.BlockSpec((None, tr, c), lambda a, i: (a, i, 0))
    return pl.pallas_call(
        _cast_kernel,
        grid=(n, r // tr),
        in_specs=[blk],
        out_specs=blk,
        out_shape=jax.ShapeDtypeStruct(w.shape, BF16),
        compiler_params=_cparams(),
        name="cast_bf16",
    )(w)


def _out_proj_kernel(y_ref, w_ref, x_ref, g_ref, *o_refs, last):
    xn = x_ref[...] + jnp.dot(y_ref[...], w_ref[...], preferred_element_type=F32)
    inv = lax.rsqrt(jnp.mean(xn * xn, axis=-1, keepdims=True) + NORM_EPS)
    hn = xn * inv * g_ref[...]
    if last:
        o_refs[0][...] = hn
    else:
        o_refs[0][...] = xn
        o_refs[1][...] = hn.astype(o_refs[1].dtype)


def _out_proj(y, w_all, j, x, gains, layer, last, tm=256):
    t, e = y.shape
    d = w_all.shape[2]
    row = pl.BlockSpec((tm, d), lambda i: (i, 0))
    if last:
        out_shape = jax.ShapeDtypeStruct((t, d), F32)
        out_specs = row
    else:
        out_shape = (jax.ShapeDtypeStruct((t, d), F32), jax.ShapeDtypeStruct((t, d), BF16))
        out_specs = (row, row)
    return pl.pallas_call(
        functools.partial(_out_proj_kernel, last=last),
        grid=(t // tm,),
        in_specs=[pl.BlockSpec((tm, e), lambda i: (i, 0)),
                  pl.BlockSpec((None, e, d), lambda i: (j, 0, 0)),
                  row,
                  pl.BlockSpec((None, 1, d), lambda i: (layer, 0, 0))],
        out_specs=out_specs,
        out_shape=out_shape,
        compiler_params=_cparams(),
        name="out_proj",
    )(y, w_all, x, gains)


def kernel(x, norm_g, final_norm_g, pool_w_in, pool_w_grp, pool_scale, pool_w_out, ssm_w_in, ssm_a_re, ssm_a_im, ssm_log_dt, ssm_b_re, ssm_b_im, ssm_c_re, ssm_c_im, ssm_d, ssm_w_glu, ssm_b_glu, ssm_w_out):
    nb, seq_len, d = x.shape
    depth = norm_g.shape[0]
    segs = LANES // nb
    ni = seq_len // (CHUNK * segs)
    assert nb * segs == LANES and ni * segs * CHUNK == seq_len
    t = nb * seq_len

    xs = x.reshape(nb, segs, ni, CHUNK, d).transpose(3, 2, 0, 1, 4).reshape(t, d)
    gains = jnp.concatenate([norm_g, final_norm_g[None]], axis=0)[:, None, :]
    pool_scale3 = pool_scale[:, None, :]
    ssm_b_glu3 = ssm_b_glu[:, None, :]
    pool_w_out_b = _cast_bf16(pool_w_out)
    ssm_w_out_b = _cast_bf16(ssm_w_out)

    ssm_tables = _ssm_tables(ssm_a_re, ssm_a_im, ssm_log_dt, ssm_b_re, ssm_b_im,
                             ssm_c_re, ssm_c_im, ssm_d, ni * CHUNK, nb)

    h = _rmsnorm(xs, gains, 0)
    for layer in range(depth):
        j = layer // 2
        if layer % 2 == 0:
            uz = _in_proj(h, pool_w_in, j)
            y = _pool_mixer(uz, pool_w_grp, pool_scale3, j, nb, ni)
            w_out = pool_w_out_b
        else:
            uz = _in_proj(h, ssm_w_in, j)
            gact = _ssm_mixer(uz, ssm_tables, j, nb, ni)
            y = _glu(gact, ssm_w_glu, ssm_b_glu3, uz, j)
            w_out = ssm_w_out_b
        last = layer == depth - 1
        res = _out_proj(y, w_out, j, xs, gains, layer + 1, last)
        if last:
            return res.reshape(CHUNK, ni, nb, segs, d).transpose(2, 3, 1, 0, 4).reshape(nb, seq_len, d)
        xs, h = res
```

```python
import functools
import math

import jax
import jax.numpy as jnp
import numpy as np
from jax import lax
from jax.experimental import pallas as pl
from jax.experimental.pallas import tpu as pltpu

F32 = jnp.float32
BF16 = jnp.bfloat16

POOL_WINDOWS = (2, 4, 8, 16)
SSM_P = 16
SSM_N = 64
CHUNK = 16
LANES = 128
GLU_SUB = 256
PACK = 16
LOG2E = math.log2(math.e)
NORM_EPS = 1e-6
VMEM_LIMIT = 56 * 1024 * 1024

_T_POW_ROWS = 4 * CHUNK
_T_BP, _T_BQ, _T_CR, _T_CI, _T_BC_ROWS = 0, 16, 32, 48, 64


def _cparams():
    return pltpu.CompilerParams(vmem_limit_bytes=VMEM_LIMIT)


def _silu(z):
    return z * jax.nn.sigmoid(z)


def _gelu_tanh(y):
    a = -2.0 * math.sqrt(2.0 / math.pi) * math.log2(math.e)
    return y / (1.0 + jnp.exp2(y * (a + (a * 0.044715) * (y * y))))


def _rmsnorm_kernel(x_ref, g_ref, o_ref):
    x = x_ref[...]
    inv = lax.rsqrt(jnp.mean(x * x, axis=-1, keepdims=True) + NORM_EPS)
    o_ref[...] = (x * inv * g_ref[...]).astype(o_ref.dtype)


def _rmsnorm(x, gains, layer, tm=512):
    t, d = x.shape
    return pl.pallas_call(
        _rmsnorm_kernel,
        grid=(t // tm,),
        in_specs=[pl.BlockSpec((tm, d), lambda i: (i, 0)),
                  pl.BlockSpec((None, 1, d), lambda i: (layer, 0, 0))],
        out_specs=pl.BlockSpec((tm, d), lambda i: (i, 0)),
        out_shape=jax.ShapeDtypeStruct((t, d), BF16),
        compiler_params=_cparams(),
        name="rmsnorm",
    )(x, gains)


def _in_proj_kernel(a_ref, w_ref, o_ref, wb_ref):
    @pl.when(pl.program_id(1) == 0)
    def _():
        wb_ref[...] = w_ref[...].astype(BF16)

    o_ref[...] = jnp.dot(a_ref[...], wb_ref[...],
                         preferred_element_type=F32).astype(o_ref.dtype)


def _in_proj(h, w_all, j, tm=1024, tn=1024):
    t, k = h.shape
    n = w_all.shape[2]
    return pl.pallas_call(
        _in_proj_kernel,
        grid=(n // tn, t // tm),
        in_specs=[pl.BlockSpec((tm, k), lambda c, r: (r, 0)),
                  pl.BlockSpec((None, k, tn), lambda c, r: (j, 0, c))],
        out_specs=pl.BlockSpec((tm, tn), lambda c, r: (r, c)),
        out_shape=jax.ShapeDtypeStruct((t, n), BF16),
        scratch_shapes=[pltpu.VMEM((k, tn), BF16)],
        compiler_params=_cparams(),
        name="in_proj",
    )(h, w_all)


def _pool_kernel(u_ref, prev_ref, z_ref, wg_ref, sc_ref, o_ref, wb_ref, hist_ref, p_ref,
                 *, w, rows, cg):
    i = pl.program_id(0)
    b = pl.program_id(1)

    @pl.when((i == 0) & (b == 0))
    def _():
        wb_ref[...] = wg_ref[...].astype(BF16)

    def strip(rs, cs, first_chunk, first_tile):
        cur = [u_ref[t, rs, cs].astype(F32) for t in range(CHUNK)]
        s = {t: cur[t] for t in range(CHUNK)}
        for m in range(1, w):
            if first_chunk:
                s[-m] = hist_ref[m - 1, rs, cs]
            else:
                s[-m] = prev_ref[CHUNK - m, rs, cs].astype(F32)
        lo, k = -(w - 1), 1
        while k < w:
            s = {t: s[t] + s[t - k] for t in range(lo + k, CHUNK)}
            lo += k
            k *= 2
        for t in range(CHUNK):
            inv = 1.0 / w
            if first_chunk and first_tile:
                row0 = lax.broadcasted_iota(jnp.int32, (PACK, 1), 0) == 0
                inv = jnp.where(row0, 1.0 / min(t + 1, w), inv)
            p_ref[t, rs, cs] = (s[t] * inv - cur[t]).astype(BF16)

    def process(first_chunk):
        for r in range(rows // PACK):
            rs = slice(r * PACK, (r + 1) * PACK)
            for c in range(cg // LANES):
                strip(rs, slice(c * LANES, (c + 1) * LANES), first_chunk, r == 0)
            p = p_ref[:, rs, :].reshape(CHUNK * PACK, cg)
            m = jnp.dot(p, wb_ref[...], preferred_element_type=F32)
            zz = z_ref[:, rs, :].reshape(CHUNK * PACK, cg).astype(F32)
            y = m * sc_ref[...] * _silu(zz)
            o_ref[:, rs, :] = y.astype(o_ref.dtype).reshape(CHUNK, PACK, cg)

    @pl.when(i == 0)
    def _():
        seg0 = lax.broadcasted_iota(jnp.int32, (rows, 1), 0) == 0
        for m in range(1, w):
            blk = prev_ref[CHUNK - m].astype(F32)
            hist_ref[m - 1] = jnp.where(seg0, 0.0, pltpu.roll(blk, 1, axis=0))
        process(True)

    @pl.when(i != 0)
    def _():
        process(False)


def _pool_mixer(uz, w_grp_all, scale_all, j, nb, ni):
    t, e2 = uz.shape
    ng, cg = w_grp_all.shape[1], w_grp_all.shape[2]
    nc = t // CHUNK
    rows = nc // (ni * nb)
    uz3 = uz.reshape(CHUNK, nc, e2)
    blk = (CHUNK, rows, cg)
    outs = []
    for g, w in enumerate(POOL_WINDOWS):
        out = pl.pallas_call(
            functools.partial(_pool_kernel, w=w, rows=rows, cg=cg),
            grid=(ni, nb),
            in_specs=[pl.BlockSpec(blk, lambda i, b, g=g: (0, i * nb + b, g)),
                      pl.BlockSpec(blk, lambda i, b, g=g: (0, ((i + ni - 1) % ni) * nb + b, g)),
                      pl.BlockSpec(blk, lambda i, b, g=g: (0, i * nb + b, ng + g)),
                      pl.BlockSpec((None, None, cg, cg), lambda i, b, g=g: (j, g, 0, 0)),
                      pl.BlockSpec((None, 1, cg), lambda i, b, g=g: (j, 0, g))],
            out_specs=pl.BlockSpec(blk, lambda i, b: (0, i * nb + b, 0)),
            out_shape=jax.ShapeDtypeStruct((CHUNK, nc, cg), BF16),
            scratch_shapes=[pltpu.VMEM((cg, cg), BF16),
                            pltpu.VMEM((max(w - 1, 1), rows, cg), F32),
                            pltpu.VMEM((CHUNK, rows, cg), BF16)],
            compiler_params=_cparams(),
            name=f"pool_mixer_w{w}",
        )(uz3, uz3, uz3, w_grp_all, scale_all)
        outs.append(out.reshape(t, cg))
    return outs


def _causal_mask():
    n = CHUNK * SSM_P
    r_tau = lax.broadcasted_iota(jnp.int32, (n, n), 0) // SSM_P
    c_tau = lax.broadcasted_iota(jnp.int32, (n, n), 1) // SSM_P
    return jnp.where(r_tau >= c_tau, 1.0, 0.0).astype(BF16)


def _gen_group_weights(tab, tbc, mask):
    def re(k):
        return jnp.broadcast_to(tab[CHUNK - k:CHUNK - k + 1, :], (SSM_P, 2 * SSM_N))

    def im(k):
        return jnp.broadcast_to(tab[3 * CHUNK - k:3 * CHUNK - k + 1, :], (SSM_P, 2 * SSM_N))

    low = lax.broadcasted_iota(jnp.int32, (SSM_P, 2 * SSM_N), 1) < SSM_N
    bp, bq = tbc[_T_BP:_T_BP + SSM_P], tbc[_T_BQ:_T_BQ + SSM_P]
    cr, ci = tbc[_T_CR:_T_CR + SSM_P], tbc[_T_CI:_T_CI + SSM_P]
    bqs = jnp.where(low, -bq, bq)
    ca = jnp.where(low, cr, -ci)
    cb_ = jnp.where(low, -ci, -cr)
    wb = jnp.concatenate([bp * re(CHUNK - 1 - t) + bqs * im(CHUNK - 1 - t) for t in range(CHUNK)], axis=0)
    rn = jnp.concatenate([bp * re(-t) + bqs * im(-t) for t in range(CHUNK)], axis=0)
    cb = [ca * re(d) + cb_ * im(d) for d in range(CHUNK + 1)]
    lm = jnp.concatenate(cb[:CHUNK], axis=0)
    wc = jnp.concatenate(cb[1:], axis=0)
    lm_h, rn_h = lm.astype(BF16), rn.astype(BF16)
    lm_l = (lm - lm_h.astype(F32)).astype(BF16)
    rn_l = (rn - rn_h.astype(F32)).astype(BF16)
    kt = lax.dot_general(jnp.concatenate([lm_h, lm_h, lm_l], axis=1),
                         jnp.concatenate([rn_h, rn_l, rn_h], axis=1),
                         (((1,), (1,)), ((), ())), preferred_element_type=F32)
    return wb.T.astype(BF16), wc.astype(BF16), kt.astype(BF16) * mask


def _ssm_kernel(u_ref, tab_ref, tbc_ref, co_ref, c1_ref, c2_ref, d_ref, o_ref, xt_ref, ot_ref,
                kw_ref, st_ref, a_ref, ep_ref, eq_ref, hp_ref, mask_ref, *, gb, nc, ni, segs):
    n = SSM_N
    kk = CHUNK * SSM_P
    gl = LANES

    @pl.when(pl.program_id(0) == 0)
    def _():
        mask_ref[...] = _causal_mask()

    for t in range(CHUNK):
        xt_ref[t] = u_ref[t].T

    def cmul(ar, ai, xr, xi):
        return ar * xr - ai * xi, ar * xi + ai * xr

    def group_rows(j):
        return pl.ds(pl.multiple_of(j * SSM_P, SSM_P), SSM_P)

    def scan_rows(j):
        return pl.ds(pl.multiple_of(j * gl, gl), gl)

    def tiles(st):
        return ([st[:n, i * LANES:(i + 1) * LANES] for i in range(ni)],
                [st[n:, i * LANES:(i + 1) * LANES] for i in range(ni)])

    def head(j, carry):
        ut = xt_ref[:, group_rows(j), :].reshape(kk, nc)
        wbt, wct, ktt = _gen_group_weights(tab_ref[j], tbc_ref[j], mask_ref[...])
        kw_ref[j, :, :kk] = ktt
        kw_ref[j, :, kk:] = wct
        st = jnp.dot(wbt, ut, preferred_element_type=F32)
        st_ref[j] = st
        s_r, s_i = tiles(st)
        co = co_ref[j]
        ar = jnp.broadcast_to(co[:, 0:1], (n, LANES))
        ai = jnp.broadcast_to(co[:, 1:2], (n, LANES))
        a_ref[j] = jnp.concatenate([ar, ai], axis=0)
        er, ei = s_r[0], s_i[0]
        for i in range(1, ni):
            mr, mi = cmul(ar, ai, er, ei)
            er, ei = mr + s_r[i], mi + s_i[i]
        ep_ref[scan_rows(j), :] = jnp.concatenate([er, ei], axis=0).T
        eq_ref[scan_rows(j), :] = jnp.concatenate([ei, er], axis=0).T
        return carry

    lax.fori_loop(0, gb, head, 0, unroll=4)

    c1, c2 = c1_ref[...], c2_ref[...]
    pairs = gb * (gl // segs)

    def seg_step(s, carry):
        p, q = carry
        at = pl.ds(s, pairs, stride=segs)
        hp_ref[at, :] = p
        return (c1 * p + c2 * q + ep_ref[at, :], c1 * q - c2 * p + eq_ref[at, :])

    zero = jnp.zeros((pairs, 2 * n), F32)
    lax.fori_loop(0, segs, seg_step, (zero, zero), unroll=8)

    def tail(j, carry):
        rows = group_rows(j)
        ut = xt_ref[:, rows, :].reshape(kk, nc)
        s_r, s_i = tiles(st_ref[j])
        a = a_ref[j]
        ar, ai = a[:n], a[n:]
        e = hp_ref[scan_rows(j), :].T
        hr, hi = e[:n], e[n:]
        h_tiles = []
        for i in range(ni):
            h_tiles.append(jnp.concatenate([hr, hi], axis=0).astype(BF16))
            if i + 1 < ni:
                mr, mi = cmul(ar, ai, hr, hi)
                hr, hi = mr + s_r[i], mi + s_i[i]
        ht = jnp.concatenate(h_tiles, axis=1)
        yt = jnp.dot(kw_ref[j], jnp.concatenate([ut, ht], axis=0), preferred_element_type=F32)
        skip = jnp.concatenate([d_ref[j]] * ni, axis=1)
        y3 = yt.reshape(CHUNK, SSM_P, nc) + skip[None] * ut.astype(F32).reshape(CHUNK, SSM_P, nc)
        ot_ref[:, rows, :] = _gelu_tanh(y3)
        return carry

    lax.fori_loop(0, gb, tail, 0, unroll=4)

    for t in range(CHUNK):
        o_ref[t] = ot_ref[t].T.astype(o_ref.dtype)


def _ssm_mixer(uz, tables, j, nb, ni, gb=8):
    tab, tbc, coef, c1s, c2s, d_tab = tables
    t, e2 = uz.shape
    e = e2 // 2
    g = e // SSM_P
    nc = t // CHUNK
    segs = LANES // nb
    uz3 = uz.reshape(CHUNK, nc, e2)
    wl = gb * SSM_P
    kern = functools.partial(_ssm_kernel, gb=gb, nc=nc, ni=ni, segs=segs)
    out = pl.pallas_call(
        kern,
        grid=(g // gb,),
        in_specs=[pl.BlockSpec((CHUNK, nc, wl), lambda i: (0, 0, i)),
                  pl.BlockSpec((None, gb, _T_POW_ROWS, 2 * SSM_N), lambda i: (j, i, 0, 0)),
                  pl.BlockSpec((None, gb, _T_BC_ROWS, 2 * SSM_N), lambda i: (j, i, 0, 0)),
                  pl.BlockSpec((None, gb, SSM_N, 2), lambda i: (j, i, 0, 0)),
                  pl.BlockSpec((None, gb * nb, 2 * SSM_N), lambda i: (j, i, 0)),
                  pl.BlockSpec((None, gb * nb, 2 * SSM_N), lambda i: (j, i, 0)),
                  pl.BlockSpec((None, gb, SSM_P, LANES), lambda i: (j, i, 0, 0))],
        out_specs=pl.BlockSpec((CHUNK, nc, wl), lambda i: (0, 0, i)),
        out_shape=jax.ShapeDtypeStruct((CHUNK, nc, e), BF16),
        scratch_shapes=[pltpu.VMEM((CHUNK, wl, nc), BF16),
                        pltpu.VMEM((CHUNK, wl, nc), F32),
                        pltpu.VMEM((gb, CHUNK * SSM_P, CHUNK * SSM_P + 2 * SSM_N), BF16),
                        pltpu.VMEM((gb, 2 * SSM_N, nc), F32),
                        pltpu.VMEM((gb, 2 * SSM_N, LANES), F32),
                        pltpu.VMEM((gb * LANES, 2 * SSM_N), F32),
                        pltpu.VMEM((gb * LANES, 2 * SSM_N), F32),
                        pltpu.VMEM((gb * LANES, 2 * SSM_N), F32),
                        pltpu.VMEM((CHUNK * SSM_P, CHUNK * SSM_P), BF16)],
        compiler_params=_cparams(),
        name="ssm_mixer",
    )(uz3, tab, tbc, coef, c1s, c2s, d_tab)
    return out.reshape(t, e)


def _ssm_tables(a_re, a_im, log_dt, b_re, b_im, c_re, c_im, d_skip, seg_tokens, nb):
    dt = jnp.exp(log_dt)[..., None]
    la, th = a_re * dt, a_im * dt
    mag = jnp.exp(la)
    abr = mag * jnp.cos(th)
    abi = mag * jnp.sin(th)
    den = a_re * a_re + a_im * a_im
    nr = abr - 1.0
    fr = (nr * a_re + abi * a_im) / den
    fi = (abi * a_re - nr * a_im) / den
    bbr = fr[..., None] * b_re - fi[..., None] * b_im
    bbi = fr[..., None] * b_im + fi[..., None] * b_re

    def cat(x, y):
        return jnp.concatenate([x, y], axis=-1)

    def power(k, la_, th_):
        m = jnp.exp(k * la_)
        return m * jnp.cos(k * th_), m * jnp.sin(k * th_)

    k = (CHUNK - np.arange(2 * CHUNK)).astype(np.float32)
    pr, pi = power(k[:, None], cat(la, la)[:, :, None, :], cat(th, th)[:, :, None, :])
    tab = jnp.concatenate([pr, pi], axis=2)

    bt_r, bt_i = jnp.swapaxes(bbr, -1, -2), jnp.swapaxes(bbi, -1, -2)
    tbc = jnp.concatenate([cat(bt_r, bt_i), cat(bt_i, bt_r),
                           cat(c_re, c_re), cat(c_im, c_im)], axis=2)

    coef = jnp.stack(power(float(CHUNK), la, th), axis=-1)
    sr, si = power(float(seg_tokens), la, th)
    c1s = jnp.repeat(cat(sr, sr), nb, axis=1)
    c2s = jnp.repeat(cat(-si, si), nb, axis=1)
    d_tab = jnp.broadcast_to(d_skip.reshape(d_skip.shape[0], -1, SSM_P, 1),
                             (d_skip.shape[0], d_skip.shape[1] // SSM_P, SSM_P, LANES))
    return tab, tbc, coef, c1s, c2s, d_tab


def _glu_kernel(a_ref, w_ref, b_ref, g_ref, z_ref, o_ref, wb_ref):
    @pl.when(pl.program_id(1) == 0)
    def _():
        wb_ref[...] = w_ref[...].astype(BF16)

    nsub = o_ref.shape[1] // GLU_SUB

    def matmul(s):
        return jnp.dot(a_ref[...], wb_ref[:, s * GLU_SUB:(s + 1) * GLU_SUB],
                       preferred_element_type=F32)

    acc = matmul(0)
    for s in range(nsub):
        cs = slice(s * GLU_SUB, (s + 1) * GLU_SUB)
        nxt = matmul(s + 1) if s + 1 < nsub else None
        gg = g_ref[:, cs].astype(F32)
        zz = z_ref[:, cs].astype(F32)
        den = (1.0 + jnp.exp2(-LOG2E * (acc + b_ref[:, cs]))) * (1.0 + jnp.exp2(-LOG2E * zz))
        o_ref[:, cs] = (gg * zz / den).astype(o_ref.dtype)
        acc = nxt


def _glu(gact, w_all, b_all, uz, j, tm=512, tn=1024):
    t, e = gact.shape
    zoff = e // tn
    return pl.pallas_call(
        _glu_kernel,
        grid=(e // tn, t // tm),
        in_specs=[pl.BlockSpec((tm, e), lambda c, r: (r, 0)),
                  pl.BlockSpec((None, e, tn), lambda c, r: (j, 0, c),
                               pipeline_mode=pl.Buffered(1)),
                  pl.BlockSpec((None, 1, tn), lambda c, r: (j, 0, c)),
                  pl.BlockSpec((tm, tn), lambda c, r: (r, c)),
                  pl.BlockSpec((tm, tn), lambda c, r: (r, c + zoff))],
        out_specs=pl.BlockSpec((tm, tn), lambda c, r: (r, c)),
        out_shape=jax.ShapeDtypeStruct((t, e), BF16),
        scratch_shapes=[pltpu.VMEM((e, tn), BF16)],
        compiler_params=_cparams(),
        name="glu",
    )(gact, w_all, b_all, gact, uz)


def _cast_kernel(w_ref, o_ref):
    o_ref[...] = w_ref[...].astype(o_ref.dtype)


def _cast_bf16(w, tr=1024):
    n, r, c = w.shape
    blk = pl.BlockSpec((None, tr, c), lambda a, i: (a, i, 0))
    return pl.pallas_call(
        _cast_kernel,
        grid=(n, r // tr),
        in_specs=[blk],
        out_specs=blk,
        out_shape=jax.ShapeDtypeStruct(w.shape, BF16),
        compiler_params=_cparams(),
        name="cast_bf16",
    )(w)


def _out_proj_kernel(*refs, last, nslab):
    y_refs, (w_ref, x_ref, g_ref), o_refs = refs[:nslab], refs[nslab:nslab + 3], refs[nslab + 3:]
    xn = x_ref[...]
    k0 = 0
    for y_ref in y_refs:
        ks = y_ref.shape[1]
        xn = xn + jnp.dot(y_ref[...], w_ref[k0:k0 + ks, :], preferred_element_type=F32)
        k0 += ks
    inv = lax.rsqrt(jnp.mean(xn * xn, axis=-1, keepdims=True) + NORM_EPS)
    hn = xn * inv * g_ref[...]
    if last:
        o_refs[0][...] = hn
    else:
        o_refs[0][...] = xn
        o_refs[1][...] = hn.astype(o_refs[1].dtype)


def _out_proj(ys, w_all, j, x, gains, layer, last, tm=256):
    t = ys[0].shape[0]
    e, d = w_all.shape[1], w_all.shape[2]
    assert sum(y.shape[1] for y in ys) == e
    row = pl.BlockSpec((tm, d), lambda i: (i, 0))
    if last:
        out_shape = jax.ShapeDtypeStruct((t, d), F32)
        out_specs = row
    else:
        out_shape = (jax.ShapeDtypeStruct((t, d), F32), jax.ShapeDtypeStruct((t, d), BF16))
        out_specs = (row, row)
    return pl.pallas_call(
        functools.partial(_out_proj_kernel, last=last, nslab=len(ys)),
        grid=(t // tm,),
        in_specs=[pl.BlockSpec((tm, y.shape[1]), lambda i: (i, 0)) for y in ys]
        + [pl.BlockSpec((None, e, d), lambda i: (j, 0, 0)),
           row,
           pl.BlockSpec((None, 1, d), lambda i: (layer, 0, 0))],
        out_specs=out_specs,
        out_shape=out_shape,
        compiler_params=_cparams(),
        name="out_proj",
    )(*ys, w_all, x, gains)


def kernel(x, norm_g, final_norm_g, pool_w_in, pool_w_grp, pool_scale, pool_w_out, ssm_w_in, ssm_a_re, ssm_a_im, ssm_log_dt, ssm_b_re, ssm_b_im, ssm_c_re, ssm_c_im, ssm_d, ssm_w_glu, ssm_b_glu, ssm_w_out):
    nb, seq_len, d = x.shape
    depth = norm_g.shape[0]
    segs = LANES // nb
    ni = seq_len // (CHUNK * segs)
    assert nb * segs == LANES and ni * segs * CHUNK == seq_len
    t = nb * seq_len

    xs = x.reshape(nb, segs, ni, CHUNK, d).transpose(3, 2, 0, 1, 4).reshape(t, d)
    gains = jnp.concatenate([norm_g, final_norm_g[None]], axis=0)[:, None, :]
    pool_scale3 = pool_scale[:, None, :]
    ssm_b_glu3 = ssm_b_glu[:, None, :]
    pool_w_out_b = _cast_bf16(pool_w_out)
    ssm_w_out_b = _cast_bf16(ssm_w_out)

    ssm_tables = _ssm_tables(ssm_a_re, ssm_a_im, ssm_log_dt, ssm_b_re, ssm_b_im,
                             ssm_c_re, ssm_c_im, ssm_d, ni * CHUNK, nb)

    h = _rmsnorm(xs, gains, 0)
    for layer in range(depth):
        j = layer // 2
        if layer % 2 == 0:
            uz = _in_proj(h, pool_w_in, j)
            ys = _pool_mixer(uz, pool_w_grp, pool_scale3, j, nb, ni)
            w_out = pool_w_out_b
        else:
            uz = _in_proj(h, ssm_w_in, j)
            gact = _ssm_mixer(uz, ssm_tables, j, nb, ni)
            ys = [_glu(gact, ssm_w_glu, ssm_b_glu3, uz, j)]
            w_out = ssm_w_out_b
        last = layer == depth - 1
        res = _out_proj(ys, w_out, j, xs, gains, layer + 1, last)
        if last:
            return res.reshape(CHUNK, ni, nb, segs, d).transpose(2, 3, 1, 0, 4).reshape(nb, seq_len, d)
        xs, h = res
```

```python
import functools
import math

import jax
import jax.numpy as jnp
import numpy as np
from jax import lax
from jax.experimental import pallas as pl
from jax.experimental.pallas import tpu as pltpu

F32 = jnp.float32
BF16 = jnp.bfloat16

POOL_WINDOWS = (2, 4, 8, 16)
SSM_P = 16
SSM_N = 64
CHUNK = 16
LANES = 128
GLU_SUB = 256
PACK = 16
LOG2E = math.log2(math.e)
NORM_EPS = 1e-6
VMEM_LIMIT = 56 * 1024 * 1024

_T_POW_ROWS = 4 * CHUNK
_T_BP, _T_BQ, _T_CR, _T_CI, _T_BC_ROWS = 0, 16, 32, 48, 64


def _cparams():
    return pltpu.CompilerParams(vmem_limit_bytes=VMEM_LIMIT)


def _silu(z):
    return z * jax.nn.sigmoid(z)


def _gelu_tanh(y):
    a = -2.0 * math.sqrt(2.0 / math.pi) * math.log2(math.e)
    return y / (1.0 + jnp.exp2(y * (a + (a * 0.044715) * (y * y))))


def _rmsnorm_kernel(x_ref, g_ref, o_ref):
    x = x_ref[...]
    inv = lax.rsqrt(jnp.mean(x * x, axis=-1, keepdims=True) + NORM_EPS)
    o_ref[...] = (x * inv * g_ref[...]).astype(o_ref.dtype)


def _rmsnorm(x, gains, layer, tm=512):
    t, d = x.shape
    return pl.pallas_call(
        _rmsnorm_kernel,
        grid=(t // tm,),
        in_specs=[pl.BlockSpec((tm, d), lambda i: (i, 0)),
                  pl.BlockSpec((None, 1, d), lambda i: (layer, 0, 0))],
        out_specs=pl.BlockSpec((tm, d), lambda i: (i, 0)),
        out_shape=jax.ShapeDtypeStruct((t, d), BF16),
        compiler_params=_cparams(),
        name="rmsnorm",
    )(x, gains)


def _in_proj_kernel(a_ref, w_ref, o_ref, wb_ref):
    @pl.when(pl.program_id(1) == 0)
    def _():
        wb_ref[...] = w_ref[...].astype(BF16)

    o_ref[...] = jnp.dot(a_ref[...], wb_ref[...],
                         preferred_element_type=F32).astype(o_ref.dtype)


def _in_proj(h, w_all, j, tm=1024, tn=1024):
    t, k = h.shape
    n = w_all.shape[2]
    return pl.pallas_call(
        _in_proj_kernel,
        grid=(n // tn, t // tm),
        in_specs=[pl.BlockSpec((tm, k), lambda c, r: (r, 0)),
                  pl.BlockSpec((None, k, tn), lambda c, r: (j, 0, c))],
        out_specs=pl.BlockSpec((tm, tn), lambda c, r: (r, c)),
        out_shape=jax.ShapeDtypeStruct((t, n), BF16),
        scratch_shapes=[pltpu.VMEM((k, tn), BF16)],
        compiler_params=_cparams(),
        name="in_proj",
    )(h, w_all)


def _pool_kernel(u_ref, wrap_ref, z_ref, wg_ref, sc_ref, o_ref, wb_ref, hist_ref, p_ref,
                 *, rows, cg):
    g = pl.program_id(0)
    b = pl.program_id(1)
    i = pl.program_id(2)

    @pl.when((i == 0) & (b == 0))
    def _():
        wb_ref[...] = wg_ref[...].astype(BF16)

    for gi, w in enumerate(POOL_WINDOWS):
        @pl.when(g == gi)
        def _(w=w):
            _pool_step(u_ref, wrap_ref, z_ref, sc_ref, o_ref, wb_ref, hist_ref, p_ref, i,
                       w=w, rows=rows, cg=cg)


def _pool_step(u_ref, wrap_ref, z_ref, sc_ref, o_ref, wb_ref, hist_ref, p_ref, i, *, w, rows, cg):
    @pl.when(i == 0)
    def _():
        seg0 = lax.broadcasted_iota(jnp.int32, (rows, 1), 0) == 0
        for m in range(1, w):
            blk = wrap_ref[CHUNK - m].astype(F32)
            hist_ref[m - 1] = jnp.where(seg0, 0.0, pltpu.roll(blk, 1, axis=0))

    def strip(rs, cs, first_tile):
        cur = [u_ref[t, rs, cs].astype(F32) for t in range(CHUNK)]
        s = {t: cur[t] for t in range(CHUNK)}
        for m in range(1, w):
            s[-m] = hist_ref[m - 1, rs, cs]
        lo, k = -(w - 1), 1
        while k < w:
            s = {t: s[t] + s[t - k] for t in range(lo + k, CHUNK)}
            lo += k
            k *= 2
        if first_tile:
            seq_start = (lax.broadcasted_iota(jnp.int32, (PACK, 1), 0) == 0) & (i == 0)
        for t in range(CHUNK):
            inv = 1.0 / w
            if first_tile:
                inv = jnp.where(seq_start, 1.0 / min(t + 1, w), inv)
            p_ref[t, rs, cs] = (s[t] * inv - cur[t]).astype(BF16)

    for r in range(rows // PACK):
        rs = slice(r * PACK, (r + 1) * PACK)
        for c in range(cg // LANES):
            strip(rs, slice(c * LANES, (c + 1) * LANES), r == 0)
        p = p_ref[:, rs, :].reshape(CHUNK * PACK, cg)
        m = jnp.dot(p, wb_ref[...], preferred_element_type=F32)
        zz = z_ref[:, rs, :].reshape(CHUNK * PACK, cg).astype(F32)
        y = m * sc_ref[...] * _silu(zz)
        o_ref[:, rs, :] = y.astype(o_ref.dtype).reshape(CHUNK, PACK, cg)

    for m in range(1, w):
        hist_ref[m - 1] = u_ref[CHUNK - m].astype(F32)


def _pool_mixer(uz, w_grp_all, scale_all, j, nb, ni):
    t, e2 = uz.shape
    e = e2 // 2
    ng, cg = w_grp_all.shape[1], w_grp_all.shape[2]
    assert ng == len(POOL_WINDOWS)
    nc = t // CHUNK
    rows = nc // (ni * nb)
    uz3 = uz.reshape(CHUNK, nc, e2)
    blk = (CHUNK, rows, cg)
    out = pl.pallas_call(
        functools.partial(_pool_kernel, rows=rows, cg=cg),
        grid=(ng, nb, ni),
        in_specs=[pl.BlockSpec(blk, lambda g, b, i: (0, i * nb + b, g)),
                  pl.BlockSpec(blk, lambda g, b, i: (0, (ni - 1) * nb + b, g)),
                  pl.BlockSpec(blk, lambda g, b, i: (0, i * nb + b, ng + g)),
                  pl.BlockSpec((None, None, cg, cg), lambda g, b, i: (j, g, 0, 0)),
                  pl.BlockSpec((None, 1, cg), lambda g, b, i: (j, 0, g))],
        out_specs=pl.BlockSpec(blk, lambda g, b, i: (0, i * nb + b, g)),
        out_shape=jax.ShapeDtypeStruct((CHUNK, nc, e), BF16),
        scratch_shapes=[pltpu.VMEM((cg, cg), BF16),
                        pltpu.VMEM((max(POOL_WINDOWS) - 1, rows, cg), F32),
                        pltpu.VMEM((CHUNK, rows, cg), BF16)],
        compiler_params=_cparams(),
        name="pool_mixer",
    )(uz3, uz3, uz3, w_grp_all, scale_all)
    return out.reshape(t, e)


def _causal_mask():
    n = CHUNK * SSM_P
    r_tau = lax.broadcasted_iota(jnp.int32, (n, n), 0) // SSM_P
    c_tau = lax.broadcasted_iota(jnp.int32, (n, n), 1) // SSM_P
    return jnp.where(r_tau >= c_tau, 1.0, 0.0).astype(BF16)


def _gen_group_weights(tab, tbc, mask):
    def re(k):
        return jnp.broadcast_to(tab[CHUNK - k:CHUNK - k + 1, :], (SSM_P, 2 * SSM_N))

    def im(k):
        return jnp.broadcast_to(tab[3 * CHUNK - k:3 * CHUNK - k + 1, :], (SSM_P, 2 * SSM_N))

    low = lax.broadcasted_iota(jnp.int32, (SSM_P, 2 * SSM_N), 1) < SSM_N
    bp, bq = tbc[_T_BP:_T_BP + SSM_P], tbc[_T_BQ:_T_BQ + SSM_P]
    cr, ci = tbc[_T_CR:_T_CR + SSM_P], tbc[_T_CI:_T_CI + SSM_P]
    bqs = jnp.where(low, -bq, bq)
    ca = jnp.where(low, cr, -ci)
    cb_ = jnp.where(low, -ci, -cr)
    wb = jnp.concatenate([bp * re(CHUNK - 1 - t) + bqs * im(CHUNK - 1 - t) for t in range(CHUNK)], axis=0)
    rn = jnp.concatenate([bp * re(-t) + bqs * im(-t) for t in range(CHUNK)], axis=0)
    cb = [ca * re(d) + cb_ * im(d) for d in range(CHUNK + 1)]
    lm = jnp.concatenate(cb[:CHUNK], axis=0)
    wc = jnp.concatenate(cb[1:], axis=0)
    lm_h, rn_h = lm.astype(BF16), rn.astype(BF16)
    lm_l = (lm - lm_h.astype(F32)).astype(BF16)
    rn_l = (rn - rn_h.astype(F32)).astype(BF16)
    kt = lax.dot_general(jnp.concatenate([lm_h, lm_h, lm_l], axis=1),
                         jnp.concatenate([rn_h, rn_l, rn_h], axis=1),
                         (((1,), (1,)), ((), ())), preferred_element_type=F32)
    return wb.T.astype(BF16), wc.astype(BF16), kt.astype(BF16) * mask


def _ssm_kernel(u_ref, tab_ref, tbc_ref, co_ref, c1_ref, c2_ref, d_ref, o_ref, xt_ref, ot_ref,
                kw_ref, st_ref, a_ref, ep_ref, eq_ref, hp_ref, mask_ref, *, gb, nc, ni, segs):
    n = SSM_N
    kk = CHUNK * SSM_P
    gl = LANES

    @pl.when(pl.program_id(0) == 0)
    def _():
        mask_ref[...] = _causal_mask()

    for t in range(CHUNK):
        xt_ref[t] = u_ref[t].T

    def cmul(ar, ai, xr, xi):
        return ar * xr - ai * xi, ar * xi + ai * xr

    def group_rows(j):
        return pl.ds(pl.multiple_of(j * SSM_P, SSM_P), SSM_P)

    def scan_rows(j):
        return pl.ds(pl.multiple_of(j * gl, gl), gl)

    def tiles(st):
        return ([st[:n, i * LANES:(i + 1) * LANES] for i in range(ni)],
                [st[n:, i * LANES:(i + 1) * LANES] for i in range(ni)])

    def head(j, carry):
        ut = xt_ref[:, group_rows(j), :].reshape(kk, nc)
        wbt, wct, ktt = _gen_group_weights(tab_ref[j], tbc_ref[j], mask_ref[...])
        kw_ref[j, :, :kk] = ktt
        kw_ref[j, :, kk:] = wct
        st = jnp.dot(wbt, ut, preferred_element_type=F32)
        st_ref[j] = st
        s_r, s_i = tiles(st)
        co = co_ref[j]
        ar = jnp.broadcast_to(co[:, 0:1], (n, LANES))
        ai = jnp.broadcast_to(co[:, 1:2], (n, LANES))
        a_ref[j] = jnp.concatenate([ar, ai], axis=0)
        er, ei = s_r[0], s_i[0]
        for i in range(1, ni):
            mr, mi = cmul(ar, ai, er, ei)
            er, ei = mr + s_r[i], mi + s_i[i]
        ep_ref[scan_rows(j), :] = jnp.concatenate([er, ei], axis=0).T
        eq_ref[scan_rows(j), :] = jnp.concatenate([ei, er], axis=0).T
        return carry

    lax.fori_loop(0, gb, head, 0, unroll=4)

    c1, c2 = c1_ref[...], c2_ref[...]
    pairs = gb * (gl // segs)

    def seg_step(s, carry):
        p, q = carry
        at = pl.ds(s, pairs, stride=segs)
        hp_ref[at, :] = p
        return (c1 * p + c2 * q + ep_ref[at, :], c1 * q - c2 * p + eq_ref[at, :])

    zero = jnp.zeros((pairs, 2 * n), F32)
    lax.fori_loop(0, segs, seg_step, (zero, zero), unroll=8)

    def tail(j, carry):
        rows = group_rows(j)
        ut = xt_ref[:, rows, :].reshape(kk, nc)
        s_r, s_i = tiles(st_ref[j])
        a = a_ref[j]
        ar, ai = a[:n], a[n:]
        e = hp_ref[scan_rows(j), :].T
        hr, hi = e[:n], e[n:]
        h_tiles = []
        for i in range(ni):
            h_tiles.append(jnp.concatenate([hr, hi], axis=0).astype(BF16))
            if i + 1 < ni:
                mr, mi = cmul(ar, ai, hr, hi)
                hr, hi = mr + s_r[i], mi + s_i[i]
        ht = jnp.concatenate(h_tiles, axis=1)
        yt = jnp.dot(kw_ref[j], jnp.concatenate([ut, ht], axis=0), preferred_element_type=F32)
        skip = jnp.concatenate([d_ref[j]] * ni, axis=1)
        y3 = yt.reshape(CHUNK, SSM_P, nc) + skip[None] * ut.astype(F32).reshape(CHUNK, SSM_P, nc)
        ot_ref[:, rows, :] = _gelu_tanh(y3)
        return carry

    lax.fori_loop(0, gb, tail, 0, unroll=4)

    for t in range(CHUNK):
        o_ref[t] = ot_ref[t].T.astype(o_ref.dtype)


def _ssm_mixer(uz, tables, j, nb, ni, gb=8):
    tab, tbc, coef, c1s, c2s, d_tab = tables
    t, e2 = uz.shape
    e = e2 // 2
    g = e // SSM_P
    nc = t // CHUNK
    segs = LANES // nb
    uz3 = uz.reshape(CHUNK, nc, e2)
    wl = gb * SSM_P
    kern = functools.partial(_ssm_kernel, gb=gb, nc=nc, ni=ni, segs=segs)
    out = pl.pallas_call(
        kern,
        grid=(g // gb,),
        in_specs=[pl.BlockSpec((CHUNK, nc, wl), lambda i: (0, 0, i)),
                  pl.BlockSpec((None, gb, _T_POW_ROWS, 2 * SSM_N), lambda i: (j, i, 0, 0)),
                  pl.BlockSpec((None, gb, _T_BC_ROWS, 2 * SSM_N), lambda i: (j, i, 0, 0)),
                  pl.BlockSpec((None, gb, SSM_N, 2), lambda i: (j, i, 0, 0)),
                  pl.BlockSpec((None, gb * nb, 2 * SSM_N), lambda i: (j, i, 0)),
                  pl.BlockSpec((None, gb * nb, 2 * SSM_N), lambda i: (j, i, 0)),
                  pl.BlockSpec((None, gb, SSM_P, LANES), lambda i: (j, i, 0, 0))],
        out_specs=pl.BlockSpec((CHUNK, nc, wl), lambda i: (0, 0, i)),
        out_shape=jax.ShapeDtypeStruct((CHUNK, nc, e), BF16),
        scratch_shapes=[pltpu.VMEM((CHUNK, wl, nc), BF16),
                        pltpu.VMEM((CHUNK, wl, nc), F32),
                        pltpu.VMEM((gb, CHUNK * SSM_P, CHUNK * SSM_P + 2 * SSM_N), BF16),
                        pltpu.VMEM((gb, 2 * SSM_N, nc), F32),
                        pltpu.VMEM((gb, 2 * SSM_N, LANES), F32),
                        pltpu.VMEM((gb * LANES, 2 * SSM_N), F32),
                        pltpu.VMEM((gb * LANES, 2 * SSM_N), F32),
                        pltpu.VMEM((gb * LANES, 2 * SSM_N), F32),
                        pltpu.VMEM((CHUNK * SSM_P, CHUNK * SSM_P), BF16)],
        compiler_params=_cparams(),
        name="ssm_mixer",
    )(uz3, tab, tbc, coef, c1s, c2s, d_tab)
    return out.reshape(t, e)


def _ssm_tables(a_re, a_im, log_dt, b_re, b_im, c_re, c_im, d_skip, seg_tokens, nb):
    dt = jnp.exp(log_dt)[..., None]
    la, th = a_re * dt, a_im * dt
    mag = jnp.exp(la)
    abr = mag * jnp.cos(th)
    abi = mag * jnp.sin(th)
    den = a_re * a_re + a_im * a_im
    nr = abr - 1.0
    fr = (nr * a_re + abi * a_im) / den
    fi = (abi * a_re - nr * a_im) / den
    bbr = fr[..., None] * b_re - fi[..., None] * b_im
    bbi = fr[..., None] * b_im + fi[..., None] * b_re

    def cat(x, y):
        return jnp.concatenate([x, y], axis=-1)

    def power(k, la_, th_):
        m = jnp.exp(k * la_)
        return m * jnp.cos(k * th_), m * jnp.sin(k * th_)

    k = (CHUNK - np.arange(2 * CHUNK)).astype(np.float32)
    pr, pi = power(k[:, None], cat(la, la)[:, :, None, :], cat(th, th)[:, :, None, :])
    tab = jnp.concatenate([pr, pi], axis=2)

    bt_r, bt_i = jnp.swapaxes(bbr, -1, -2), jnp.swapaxes(bbi, -1, -2)
    tbc = jnp.concatenate([cat(bt_r, bt_i), cat(bt_i, bt_r),
                           cat(c_re, c_re), cat(c_im, c_im)], axis=2)

    coef = jnp.stack(power(float(CHUNK), la, th), axis=-1)
    sr, si = power(float(seg_tokens), la, th)
    c1s = jnp.repeat(cat(sr, sr), nb, axis=1)
    c2s = jnp.repeat(cat(-si, si), nb, axis=1)
    d_tab = jnp.broadcast_to(d_skip.reshape(d_skip.shape[0], -1, SSM_P, 1),
                             (d_skip.shape[0], d_skip.shape[1] // SSM_P, SSM_P, LANES))
    return tab, tbc, coef, c1s, c2s, d_tab


def _glu_kernel(a_ref, w_ref, b_ref, g_ref, z_ref, o_ref, wb_ref):
    @pl.when(pl.program_id(1) == 0)
    def _():
        wb_ref[...] = w_ref[...].astype(BF16)

    nsub = o_ref.shape[1] // GLU_SUB

    def matmul(s):
        return jnp.dot(a_ref[...], wb_ref[:, s * GLU_SUB:(s + 1) * GLU_SUB],
                       preferred_element_type=F32)

    acc = matmul(0)
    for s in range(nsub):
        cs = slice(s * GLU_SUB, (s + 1) * GLU_SUB)
        nxt = matmul(s + 1) if s + 1 < nsub else None
        gg = g_ref[:, cs].astype(F32)
        zz = z_ref[:, cs].astype(F32)
        den = (1.0 + jnp.exp2(-LOG2E * (acc + b_ref[:, cs]))) * (1.0 + jnp.exp2(-LOG2E * zz))
        o_ref[:, cs] = (gg * zz / den).astype(o_ref.dtype)
        acc = nxt


def _glu(gact, w_all, b_all, uz, j, tm=512, tn=1024):
    t, e = gact.shape
    zoff = e // tn
    return pl.pallas_call(
        _glu_kernel,
        grid=(e // tn, t // tm),
        in_specs=[pl.BlockSpec((tm, e), lambda c, r: (r, 0)),
                  pl.BlockSpec((None, e, tn), lambda c, r: (j, 0, c),
                               pipeline_mode=pl.Buffered(1)),
                  pl.BlockSpec((None, 1, tn), lambda c, r: (j, 0, c)),
                  pl.BlockSpec((tm, tn), lambda c, r: (r, c)),
                  pl.BlockSpec((tm, tn), lambda c, r: (r, c + zoff))],
        out_specs=pl.BlockSpec((tm, tn), lambda c, r: (r, c)),
        out_shape=jax.ShapeDtypeStruct((t, e), BF16),
        scratch_shapes=[pltpu.VMEM((e, tn), BF16)],
        compiler_params=_cparams(),
        name="glu",
    )(gact, w_all, b_all, gact, uz)


def _cast_kernel(w_ref, o_ref):
    o_ref[...] = w_ref[...].astype(o_ref.dtype)


def _cast_bf16(w, tr=1024):
    n, r, c = w.shape
    blk = pl.BlockSpec((None, tr, c), lambda a, i: (a, i, 0))
    return pl.pallas_call(
        _cast_kernel,
        grid=(n, r // tr),
        in_specs=[blk],
        out_specs=blk,
        out_shape=jax.ShapeDtypeStruct(w.shape, BF16),
        compiler_params=_cparams(),
        name="cast_bf16",
    )(w)


def _out_proj_kernel(*refs, last, nslab):
    y_refs, (w_ref, x_ref, g_ref), o_refs = refs[:nslab], refs[nslab:nslab + 3], refs[nslab + 3:]
    xn = x_ref[...]
    k0 = 0
    for y_ref in y_refs:
        ks = y_ref.shape[1]
        xn = xn + jnp.dot(y_ref[...], w_ref[k0:k0 + ks, :], preferred_element_type=F32)
        k0 += ks
    inv = lax.rsqrt(jnp.mean(xn * xn, axis=-1, keepdims=True) + NORM_EPS)
    hn = xn * inv * g_ref[...]
    if last:
        o_refs[0][...] = hn
    else:
        o_refs[0][...] = xn
        o_refs[1][...] = hn.astype(o_refs[1].dtype)


def _out_proj(ys, w_all, j, x, gains, layer, last, tm=256):
    t = ys[0].shape[0]
    e, d = w_all.shape[1], w_all.shape[2]
    assert sum(y.shape[1] for y in ys) == e
    row = pl.BlockSpec((tm, d), lambda i: (i, 0))
    if last:
        out_shape = jax.ShapeDtypeStruct((t, d), F32)
        out_specs = row
    else:
        out_shape = (jax.ShapeDtypeStruct((t, d), F32), jax.ShapeDtypeStruct((t, d), BF16))
        out_specs = (row, row)
    return pl.pallas_call(
        functools.partial(_out_proj_kernel, last=last, nslab=len(ys)),
        grid=(t // tm,),
        in_specs=[pl.BlockSpec((tm, y.shape[1]), lambda i: (i, 0)) for y in ys]
        + [pl.BlockSpec((None, e, d), lambda i: (j, 0, 0)),
           row,
           pl.BlockSpec((None, 1, d), lambda i: (layer, 0, 0))],
        out_specs=out_specs,
        out_shape=out_shape,
        compiler_params=_cparams(),
        name="out_proj",
    )(*ys, w_all, x, gains)


def kernel(x, norm_g, final_norm_g, pool_w_in, pool_w_grp, pool_scale, pool_w_out, ssm_w_in, ssm_a_re, ssm_a_im, ssm_log_dt, ssm_b_re, ssm_b_im, ssm_c_re, ssm_c_im, ssm_d, ssm_w_glu, ssm_b_glu, ssm_w_out):
    nb, seq_len, d = x.shape
    depth = norm_g.shape[0]
    segs = LANES // nb
    ni = seq_len // (CHUNK * segs)
    assert nb * segs == LANES and ni * segs * CHUNK == seq_len
    t = nb * seq_len

    xs = x.reshape(nb, segs, ni, CHUNK, d).transpose(3, 2, 0, 1, 4).reshape(t, d)
    gains = jnp.concatenate([norm_g, final_norm_g[None]], axis=0)[:, None, :]
    pool_scale3 = pool_scale[:, None, :]
    ssm_b_glu3 = ssm_b_glu[:, None, :]
    pool_w_out_b = _cast_bf16(pool_w_out)
    ssm_w_out_b = _cast_bf16(ssm_w_out)

    ssm_tables = _ssm_tables(ssm_a_re, ssm_a_im, ssm_log_dt, ssm_b_re, ssm_b_im,
                             ssm_c_re, ssm_c_im, ssm_d, ni * CHUNK, nb)

    h = _rmsnorm(xs, gains, 0)
    for layer in range(depth):
        j = layer // 2
        if layer % 2 == 0:
            uz = _in_proj(h, pool_w_in, j)
            ys = [_pool_mixer(uz, pool_w_grp, pool_scale3, j, nb, ni)]
            w_out = pool_w_out_b
        else:
            uz = _in_proj(h, ssm_w_in, j)
            gact = _ssm_mixer(uz, ssm_tables, j, nb, ni)
            ys = [_glu(gact, ssm_w_glu, ssm_b_glu3, uz, j)]
            w_out = ssm_w_out_b
        last = layer == depth - 1
        res = _out_proj(ys, w_out, j, xs, gains, layer + 1, last)
        if last:
            return res.reshape(CHUNK, ni, nb, segs, d).transpose(2, 3, 1, 0, 4).reshape(nb, seq_len, d)
        xs, h = res
```

```python
import functools
import math

import jax
import jax.numpy as jnp
import numpy as np
from jax import lax
from jax.experimental import pallas as pl
from jax.experimental.pallas import tpu as pltpu

F32 = jnp.float32
BF16 = jnp.bfloat16

POOL_WINDOWS = (2, 4, 8, 16)
SSM_P = 16
SSM_N = 64
CHUNK = 16
LANES = 128
GLU_SUB = 256
PACK = 16
LOG2E = math.log2(math.e)
NORM_EPS = 1e-6
VMEM_LIMIT = 56 * 1024 * 1024

_T_POW_ROWS = 4 * CHUNK
_T_BP, _T_BQ, _T_CR, _T_CI, _T_BC_ROWS = 0, 16, 32, 48, 64


def _cparams():
    return pltpu.CompilerParams(vmem_limit_bytes=VMEM_LIMIT)


def _silu(z):
    return z * jax.nn.sigmoid(z)


def _gelu_tanh(y):
    a = -2.0 * math.sqrt(2.0 / math.pi) * math.log2(math.e)
    return y / (1.0 + jnp.exp2(y * (a + (a * 0.044715) * (y * y))))


def _in_proj_kernel(a_ref, g_ref, w_ref, wo_ref, o_ref, wo_b_ref, wb_ref, *, norm):
    @pl.when(pl.program_id(1) == 0)
    def _():
        wb_ref[...] = w_ref[...].astype(BF16)

    a = a_ref[...]
    if norm:
        inv = lax.rsqrt(jnp.mean(a * a, axis=-1, keepdims=True) + NORM_EPS)
        a = (a * inv * g_ref[...]).astype(BF16)
    o_ref[...] = jnp.dot(a, wb_ref[...], preferred_element_type=F32).astype(o_ref.dtype)
    wo_b_ref[...] = wo_ref[...].astype(BF16)


def _in_proj(h, gains, layer, w_all, w_out_all, j, tm=1024, tn=1024):
    t, k = h.shape
    n = w_all.shape[2]
    e, d = w_out_all.shape[1], w_out_all.shape[2]
    nr = t // tm
    steps = (n // tn) * nr
    ws = e // steps
    return pl.pallas_call(
        functools.partial(_in_proj_kernel, norm=h.dtype == F32),
        grid=(n // tn, nr),
        in_specs=[pl.BlockSpec((tm, k), lambda c, r: (r, 0)),
                  pl.BlockSpec((None, 1, k), lambda c, r: (layer, 0, 0)),
                  pl.BlockSpec((None, k, tn), lambda c, r: (j, 0, c)),
                  pl.BlockSpec((None, ws, d), lambda c, r: (j, c * nr + r, 0))],
        out_specs=(pl.BlockSpec((tm, tn), lambda c, r: (r, c)),
                   pl.BlockSpec((ws, d), lambda c, r: (c * nr + r, 0))),
        out_shape=(jax.ShapeDtypeStruct((t, n), BF16), jax.ShapeDtypeStruct((e, d), BF16)),
        scratch_shapes=[pltpu.VMEM((k, tn), BF16)],
        compiler_params=_cparams(),
        name="in_proj",
    )(h, gains, w_all, w_out_all)


def _pool_kernel(u_ref, wrap_ref, z_ref, wg_ref, sc_ref, o_ref, wb_ref, hist_ref, p_ref,
                 *, rows, cg):
    g = pl.program_id(0)
    b = pl.program_id(1)
    i = pl.program_id(2)

    @pl.when((i == 0) & (b == 0))
    def _():
        wb_ref[...] = wg_ref[...].astype(BF16)

    for gi, w in enumerate(POOL_WINDOWS):
        @pl.when(g == gi)
        def _(w=w):
            _pool_step(u_ref, wrap_ref, z_ref, sc_ref, o_ref, wb_ref, hist_ref, p_ref, i,
                       w=w, rows=rows, cg=cg)


def _pool_step(u_ref, wrap_ref, z_ref, sc_ref, o_ref, wb_ref, hist_ref, p_ref, i, *, w, rows, cg):
    @pl.when(i == 0)
    def _():
        seg0 = lax.broadcasted_iota(jnp.int32, (rows, 1), 0) == 0
        for m in range(1, w):
            blk = wrap_ref[CHUNK - m].astype(F32)
            hist_ref[m - 1] = jnp.where(seg0, 0.0, pltpu.roll(blk, 1, axis=0))

    def strip(rs, cs, first_tile):
        cur = [u_ref[t, rs, cs].astype(F32) for t in range(CHUNK)]
        s = {t: cur[t] for t in range(CHUNK)}
        for m in range(1, w):
            s[-m] = hist_ref[m - 1, rs, cs]
        lo, k = -(w - 1), 1
        while k < w:
            s = {t: s[t] + s[t - k] for t in range(lo + k, CHUNK)}
            lo += k
            k *= 2
        if first_tile:
            seq_start = (lax.broadcasted_iota(jnp.int32, (PACK, 1), 0) == 0) & (i == 0)
        for t in range(CHUNK):
            inv = 1.0 / w
            if first_tile:
                inv = jnp.where(seq_start, 1.0 / min(t + 1, w), inv)
            p_ref[t, rs, cs] = (s[t] * inv - cur[t]).astype(BF16)

    for r in range(rows // PACK):
        rs = slice(r * PACK, (r + 1) * PACK)
        for c in range(cg // LANES):
            strip(rs, slice(c * LANES, (c + 1) * LANES), r == 0)
        p = p_ref[:, rs, :].reshape(CHUNK * PACK, cg)
        m = jnp.dot(p, wb_ref[...], preferred_element_type=F32)
        zz = z_ref[:, rs, :].reshape(CHUNK * PACK, cg).astype(F32)
        y = m * sc_ref[...] * _silu(zz)
        o_ref[:, rs, :] = y.astype(o_ref.dtype).reshape(CHUNK, PACK, cg)

    for m in range(1, w):
        hist_ref[m - 1] = u_ref[CHUNK - m].astype(F32)


def _pool_mixer(uz, w_grp_all, scale_all, j, nb, ni):
    t, e2 = uz.shape
    e = e2 // 2
    ng, cg = w_grp_all.shape[1], w_grp_all.shape[2]
    assert ng == len(POOL_WINDOWS)
    nc = t // CHUNK
    rows = nc // (ni * nb)
    uz3 = uz.reshape(CHUNK, nc, e2)
    blk = (CHUNK, rows, cg)
    out = pl.pallas_call(
        functools.partial(_pool_kernel, rows=rows, cg=cg),
        grid=(ng, nb, ni),
        in_specs=[pl.BlockSpec(blk, lambda g, b, i: (0, i * nb + b, g)),
                  pl.BlockSpec(blk, lambda g, b, i: (0, (ni - 1) * nb + b, g)),
                  pl.BlockSpec(blk, lambda g, b, i: (0, i * nb + b, ng + g)),
                  pl.BlockSpec((None, None, cg, cg), lambda g, b, i: (j, g, 0, 0)),
                  pl.BlockSpec((None, 1, cg), lambda g, b, i: (j, 0, g))],
        out_specs=pl.BlockSpec(blk, lambda g, b, i: (0, i * nb + b, g)),
        out_shape=jax.ShapeDtypeStruct((CHUNK, nc, e), BF16),
        scratch_shapes=[pltpu.VMEM((cg, cg), BF16),
                        pltpu.VMEM((max(POOL_WINDOWS) - 1, rows, cg), F32),
                        pltpu.VMEM((CHUNK, rows, cg), BF16)],
        compiler_params=_cparams(),
        name="pool_mixer",
    )(uz3, uz3, uz3, w_grp_all, scale_all)
    return out.reshape(t, e)


def _causal_mask():
    n = CHUNK * SSM_P
    r_tau = lax.broadcasted_iota(jnp.int32, (n, n), 0) // SSM_P
    c_tau = lax.broadcasted_iota(jnp.int32, (n, n), 1) // SSM_P
    return jnp.where(r_tau >= c_tau, 1.0, 0.0).astype(BF16)


def _gen_group_weights(tab, tbc, mask):
    def re(k):
        return jnp.broadcast_to(tab[CHUNK - k:CHUNK - k + 1, :], (SSM_P, 2 * SSM_N))

    def im(k):
        return jnp.broadcast_to(tab[3 * CHUNK - k:3 * CHUNK - k + 1, :], (SSM_P, 2 * SSM_N))

    low = lax.broadcasted_iota(jnp.int32, (SSM_P, 2 * SSM_N), 1) < SSM_N
    bp, bq = tbc[_T_BP:_T_BP + SSM_P], tbc[_T_BQ:_T_BQ + SSM_P]
    cr, ci = tbc[_T_CR:_T_CR + SSM_P], tbc[_T_CI:_T_CI + SSM_P]
    bqs = jnp.where(low, -bq, bq)
    ca = jnp.where(low, cr, -ci)
    cb_ = jnp.where(low, -ci, -cr)
    wb = jnp.concatenate([bp * re(CHUNK - 1 - t) + bqs * im(CHUNK - 1 - t) for t in range(CHUNK)], axis=0)
    rn = jnp.concatenate([bp * re(-t) + bqs * im(-t) for t in range(CHUNK)], axis=0)
    cb = [ca * re(d) + cb_ * im(d) for d in range(CHUNK + 1)]
    lm = jnp.concatenate(cb[:CHUNK], axis=0)
    wc = jnp.concatenate(cb[1:], axis=0)
    lm_h, rn_h = lm.astype(BF16), rn.astype(BF16)
    lm_l = (lm - lm_h.astype(F32)).astype(BF16)
    rn_l = (rn - rn_h.astype(F32)).astype(BF16)
    kt = lax.dot_general(jnp.concatenate([lm_h, lm_h, lm_l], axis=1),
                         jnp.concatenate([rn_h, rn_l, rn_h], axis=1),
                         (((1,), (1,)), ((), ())), preferred_element_type=F32)
    return wb.T.astype(BF16), wc.astype(BF16), kt.astype(BF16) * mask


def _ssm_kernel(u_ref, tab_ref, tbc_ref, co_ref, c1_ref, c2_ref, d_ref, o_ref, xt_ref, ot_ref,
                kw_ref, st_ref, a_ref, ep_ref, eq_ref, hp_ref, mask_ref, *, gb, nc, ni, segs):
    n = SSM_N
    kk = CHUNK * SSM_P
    gl = LANES

    @pl.when(pl.program_id(0) == 0)
    def _():
        mask_ref[...] = _causal_mask()

    for t in range(CHUNK):
        xt_ref[t] = u_ref[t].T

    def cmul(ar, ai, xr, xi):
        return ar * xr - ai * xi, ar * xi + ai * xr

    def group_rows(j):
        return pl.ds(pl.multiple_of(j * SSM_P, SSM_P), SSM_P)

    def scan_rows(j):
        return pl.ds(pl.multiple_of(j * gl, gl), gl)

    def tiles(st):
        return ([st[:n, i * LANES:(i + 1) * LANES] for i in range(ni)],
                [st[n:, i * LANES:(i + 1) * LANES] for i in range(ni)])

    def head(j, carry):
        ut = xt_ref[:, group_rows(j), :].reshape(kk, nc)
        wbt, wct, ktt = _gen_group_weights(tab_ref[j], tbc_ref[j], mask_ref[...])
        kw_ref[j, :, :kk] = ktt
        kw_ref[j, :, kk:] = wct
        st = jnp.dot(wbt, ut, preferred_element_type=F32)
        st_ref[j] = st
        s_r, s_i = tiles(st)
        co = co_ref[j]
        ar = jnp.broadcast_to(co[:, 0:1], (n, LANES))
        ai = jnp.broadcast_to(co[:, 1:2], (n, LANES))
        a_ref[j] = jnp.concatenate([ar, ai], axis=0)
        er, ei = s_r[0], s_i[0]
        for i in range(1, ni):
            mr, mi = cmul(ar, ai, er, ei)
            er, ei = mr + s_r[i], mi + s_i[i]
        ep_ref[scan_rows(j), :] = jnp.concatenate([er, ei], axis=0).T
        eq_ref[scan_rows(j), :] = jnp.concatenate([ei, er], axis=0).T
        return carry

    lax.fori_loop(0, gb, head, 0, unroll=4)

    c1, c2 = c1_ref[...], c2_ref[...]
    pairs = gb * (gl // segs)

    def seg_step(s, carry):
        p, q = carry
        at = pl.ds(s, pairs, stride=segs)
        hp_ref[at, :] = p
        return (c1 * p + c2 * q + ep_ref[at, :], c1 * q - c2 * p + eq_ref[at, :])

    zero = jnp.zeros((pairs, 2 * n), F32)
    lax.fori_loop(0, segs, seg_step, (zero, zero), unroll=8)

    def tail(j, carry):
        rows = group_rows(j)
        ut = xt_ref[:, rows, :].reshape(kk, nc)
        s_r, s_i = tiles(st_ref[j])
        a = a_ref[j]
        ar, ai = a[:n], a[n:]
        e = hp_ref[scan_rows(j), :].T
        hr, hi = e[:n], e[n:]
        h_tiles = []
        for i in range(ni):
            h_tiles.append(jnp.concatenate([hr, hi], axis=0).astype(BF16))
            if i + 1 < ni:
                mr, mi = cmul(ar, ai, hr, hi)
                hr, hi = mr + s_r[i], mi + s_i[i]
        ht = jnp.concatenate(h_tiles, axis=1)
        yt = jnp.dot(kw_ref[j], jnp.concatenate([ut, ht], axis=0), preferred_element_type=F32)
        skip = jnp.concatenate([d_ref[j]] * ni, axis=1)
        y3 = yt.reshape(CHUNK, SSM_P, nc) + skip[None] * ut.astype(F32).reshape(CHUNK, SSM_P, nc)
        ot_ref[:, rows, :] = _gelu_tanh(y3)
        return carry

    lax.fori_loop(0, gb, tail, 0, unroll=4)

    for t in range(CHUNK):
        o_ref[t] = ot_ref[t].T.astype(o_ref.dtype)


def _ssm_mixer(uz, tables, j, nb, ni, gb=8):
    tab, tbc, coef, c1s, c2s, d_tab = tables
    t, e2 = uz.shape
    e = e2 // 2
    g = e // SSM_P
    nc = t // CHUNK
    segs = LANES // nb
    uz3 = uz.reshape(CHUNK, nc, e2)
    wl = gb * SSM_P
    kern = functools.partial(_ssm_kernel, gb=gb, nc=nc, ni=ni, segs=segs)
    out = pl.pallas_call(
        kern,
        grid=(g // gb,),
        in_specs=[pl.BlockSpec((CHUNK, nc, wl), lambda i: (0, 0, i)),
                  pl.BlockSpec((None, gb, _T_POW_ROWS, 2 * SSM_N), lambda i: (j, i, 0, 0)),
                  pl.BlockSpec((None, gb, _T_BC_ROWS, 2 * SSM_N), lambda i: (j, i, 0, 0)),
                  pl.BlockSpec((None, gb, SSM_N, 2), lambda i: (j, i, 0, 0)),
                  pl.BlockSpec((None, gb * nb, 2 * SSM_N), lambda i: (j, i, 0)),
                  pl.BlockSpec((None, gb * nb, 2 * SSM_N), lambda i: (j, i, 0)),
                  pl.BlockSpec((None, gb, SSM_P, LANES), lambda i: (j, i, 0, 0))],
        out_specs=pl.BlockSpec((CHUNK, nc, wl), lambda i: (0, 0, i)),
        out_shape=jax.ShapeDtypeStruct((CHUNK, nc, e), BF16),
        scratch_shapes=[pltpu.VMEM((CHUNK, wl, nc), BF16),
                        pltpu.VMEM((CHUNK, wl, nc), F32),
                        pltpu.VMEM((gb, CHUNK * SSM_P, CHUNK * SSM_P + 2 * SSM_N), BF16),
                        pltpu.VMEM((gb, 2 * SSM_N, nc), F32),
                        pltpu.VMEM((gb, 2 * SSM_N, LANES), F32),
                        pltpu.VMEM((gb * LANES, 2 * SSM_N), F32),
                        pltpu.VMEM((gb * LANES, 2 * SSM_N), F32),
                        pltpu.VMEM((gb * LANES, 2 * SSM_N), F32),
                        pltpu.VMEM((CHUNK * SSM_P, CHUNK * SSM_P), BF16)],
        compiler_params=_cparams(),
        name="ssm_mixer",
    )(uz3, tab, tbc, coef, c1s, c2s, d_tab)
    return out.reshape(t, e)


def _ssm_tables(a_re, a_im, log_dt, b_re, b_im, c_re, c_im, d_skip, seg_tokens, nb):
    dt = jnp.exp(log_dt)[..., None]
    la, th = a_re * dt, a_im * dt
    mag = jnp.exp(la)
    abr = mag * jnp.cos(th)
    abi = mag * jnp.sin(th)
    den = a_re * a_re + a_im * a_im
    nr = abr - 1.0
    fr = (nr * a_re + abi * a_im) / den
    fi = (abi * a_re - nr * a_im) / den
    bbr = fr[..., None] * b_re - fi[..., None] * b_im
    bbi = fr[..., None] * b_im + fi[..., None] * b_re

    def cat(x, y):
        return jnp.concatenate([x, y], axis=-1)

    def power(k, la_, th_):
        m = jnp.exp(k * la_)
        return m * jnp.cos(k * th_), m * jnp.sin(k * th_)

    k = (CHUNK - np.arange(2 * CHUNK)).astype(np.float32)
    pr, pi = power(k[:, None], cat(la, la)[:, :, None, :], cat(th, th)[:, :, None, :])
    tab = jnp.concatenate([pr, pi], axis=2)

    bt_r, bt_i = jnp.swapaxes(bbr, -1, -2), jnp.swapaxes(bbi, -1, -2)
    tbc = jnp.concatenate([cat(bt_r, bt_i), cat(bt_i, bt_r),
                           cat(c_re, c_re), cat(c_im, c_im)], axis=2)

    coef = jnp.stack(power(float(CHUNK), la, th), axis=-1)
    sr, si = power(float(seg_tokens), la, th)
    c1s = jnp.repeat(cat(sr, sr), nb, axis=1)
    c2s = jnp.repeat(cat(-si, si), nb, axis=1)
    d_tab = jnp.broadcast_to(d_skip.reshape(d_skip.shape[0], -1, SSM_P, 1),
                             (d_skip.shape[0], d_skip.shape[1] // SSM_P, SSM_P, LANES))
    return tab, tbc, coef, c1s, c2s, d_tab


def _glu_kernel(a_ref, w_ref, b_ref, g_ref, z_ref, o_ref, wb_ref):
    @pl.when(pl.program_id(1) == 0)
    def _():
        wb_ref[...] = w_ref[...].astype(BF16)

    nsub = o_ref.shape[1] // GLU_SUB

    def matmul(s):
        return jnp.dot(a_ref[...], wb_ref[:, s * GLU_SUB:(s + 1) * GLU_SUB],
                       preferred_element_type=F32)

    acc = matmul(0)
    for s in range(nsub):
        cs = slice(s * GLU_SUB, (s + 1) * GLU_SUB)
        nxt = matmul(s + 1) if s + 1 < nsub else None
        gg = g_ref[:, cs].astype(F32)
        zz = z_ref[:, cs].astype(F32)
        den = (1.0 + jnp.exp2(-LOG2E * (acc + b_ref[:, cs]))) * (1.0 + jnp.exp2(-LOG2E * zz))
        o_ref[:, cs] = (gg * zz / den).astype(o_ref.dtype)
        acc = nxt


def _glu(gact, w_all, b_all, uz, j, tm=512, tn=1024):
    t, e = gact.shape
    zoff = e // tn
    return pl.pallas_call(
        _glu_kernel,
        grid=(e // tn, t // tm),
        in_specs=[pl.BlockSpec((tm, e), lambda c, r: (r, 0)),
                  pl.BlockSpec((None, e, tn), lambda c, r: (j, 0, c),
                               pipeline_mode=pl.Buffered(1)),
                  pl.BlockSpec((None, 1, tn), lambda c, r: (j, 0, c)),
                  pl.BlockSpec((tm, tn), lambda c, r: (r, c)),
                  pl.BlockSpec((tm, tn), lambda c, r: (r, c + zoff))],
        out_specs=pl.BlockSpec((tm, tn), lambda c, r: (r, c)),
        out_shape=jax.ShapeDtypeStruct((t, e), BF16),
        scratch_shapes=[pltpu.VMEM((e, tn), BF16)],
        compiler_params=_cparams(),
        name="glu",
    )(gact, w_all, b_all, gact, uz)


def _out_proj_kernel(y_ref, w_ref, x_ref, g_ref, *o_refs, last):
    xn = x_ref[...] + jnp.dot(y_ref[...], w_ref[...], preferred_element_type=F32)
    inv = lax.rsqrt(jnp.mean(xn * xn, axis=-1, keepdims=True) + NORM_EPS)
    hn = xn * inv * g_ref[...]
    if last:
        o_refs[0][...] = hn
    else:
        o_refs[0][...] = xn
        o_refs[1][...] = hn.astype(o_refs[1].dtype)


def _out_proj(y, w, x, gains, layer, last, tm=256):
    t, e = y.shape
    d = w.shape[1]
    row = pl.BlockSpec((tm, d), lambda i: (i, 0))
    if last:
        out_shape = jax.ShapeDtypeStruct((t, d), F32)
        out_specs = row
    else:
        out_shape = (jax.ShapeDtypeStruct((t, d), F32), jax.ShapeDtypeStruct((t, d), BF16))
        out_specs = (row, row)
    return pl.pallas_call(
        functools.partial(_out_proj_kernel, last=last),
        grid=(t // tm,),
        in_specs=[pl.BlockSpec((tm, e), lambda i: (i, 0)),
                  pl.BlockSpec((e, d), lambda i: (0, 0)),
                  row,
                  pl.BlockSpec((None, 1, d), lambda i: (layer, 0, 0))],
        out_specs=out_specs,
        out_shape=out_shape,
        compiler_params=_cparams(),
        name="out_proj",
    )(y, w, x, gains)


def kernel(x, norm_g, final_norm_g, pool_w_in, pool_w_grp, pool_scale, pool_w_out, ssm_w_in, ssm_a_re, ssm_a_im, ssm_log_dt, ssm_b_re, ssm_b_im, ssm_c_re, ssm_c_im, ssm_d, ssm_w_glu, ssm_b_glu, ssm_w_out):
    nb, seq_len, d = x.shape
    depth = norm_g.shape[0]
    segs = LANES // nb
    ni = seq_len // (CHUNK * segs)
    assert nb * segs == LANES and ni * segs * CHUNK == seq_len
    t = nb * seq_len

    gains = jnp.concatenate([norm_g, final_norm_g[None]], axis=0)[:, None, :]
    pool_scale3 = pool_scale[:, None, :]
    ssm_b_glu3 = ssm_b_glu[:, None, :]
    ssm_tables = _ssm_tables(ssm_a_re, ssm_a_im, ssm_log_dt, ssm_b_re, ssm_b_im,
                             ssm_c_re, ssm_c_im, ssm_d, ni * CHUNK, nb)

    xs = x.reshape(nb, segs, ni, CHUNK, d).transpose(3, 2, 0, 1, 4).reshape(t, d)
    h = xs
    for layer in range(depth):
        j = layer // 2
        if layer % 2 == 0:
            uz, w_out = _in_proj(h, gains, layer, pool_w_in, pool_w_out, j)
            y = _pool_mixer(uz, pool_w_grp, pool_scale3, j, nb, ni)
        else:
            uz, w_out = _in_proj(h, gains, layer, ssm_w_in, ssm_w_out, j)
            gact = _ssm_mixer(uz, ssm_tables, j, nb, ni)
            y = _glu(gact, ssm_w_glu, ssm_b_glu3, uz, j)
        last = layer == depth - 1
        res = _out_proj(y, w_out, xs, gains, layer + 1, last)
        if last:
            return res.reshape(CHUNK, ni, nb, segs, d).transpose(2, 3, 1, 0, 4).reshape(nb, seq_len, d)
        xs, h = res
```

```python
import functools
import math

import jax
import jax.numpy as jnp
import numpy as np
from jax import lax
from jax.experimental import pallas as pl
from jax.experimental.pallas import tpu as pltpu

F32 = jnp.float32
BF16 = jnp.bfloat16

POOL_WINDOWS = (2, 4, 8, 16)
SSM_P = 16
SSM_N = 64
CHUNK = 16
LANES = 128
GLU_SUB = 256
PACK = 16
LOG2E = math.log2(math.e)
NORM_EPS = 1e-6
VMEM_LIMIT = 56 * 1024 * 1024

_T_POW_ROWS = 4 * CHUNK
_T_BP, _T_BQ, _T_CR, _T_CI, _T_BC_ROWS = 0, 16, 32, 48, 64


def _cparams():
    return pltpu.CompilerParams(vmem_limit_bytes=VMEM_LIMIT)


def _silu(z):
    return z * jax.nn.sigmoid(z)


def _gelu_tanh(y):
    a = -2.0 * math.sqrt(2.0 / math.pi) * math.log2(math.e)
    return y / (1.0 + jnp.exp2(y * (a + (a * 0.044715) * (y * y))))


def _in_proj_kernel(*refs, norm, ncast):
    a_ref, g_ref, w_ref = refs[:3]
    cast_in, o_ref = refs[3:3 + ncast], refs[3 + ncast]
    cast_out, wb_ref = refs[4 + ncast:4 + 2 * ncast], refs[4 + 2 * ncast]

    @pl.when(pl.program_id(1) == 0)
    def _():
        wb_ref[...] = w_ref[...].astype(BF16)

    a = a_ref[...]
    if norm:
        inv = lax.rsqrt(jnp.mean(a * a, axis=-1, keepdims=True) + NORM_EPS)
        a = (a * inv * g_ref[...]).astype(BF16)
    o_ref[...] = jnp.dot(a, wb_ref[...], preferred_element_type=F32).astype(o_ref.dtype)
    for src, dst in zip(cast_in, cast_out):
        dst[...] = src[...].astype(BF16)


def _in_proj(h, gains, layer, w_all, j, later_weights, tm=1024, tn=1024):
    t, k = h.shape
    n = w_all.shape[2]
    nr = t // tm
    steps = (n // tn) * nr
    cast_specs_in, cast_specs_out, cast_shapes = [], [], []
    for w in later_weights:
        rows, cols = w.shape[1], w.shape[2]
        ws = rows // steps
        cast_specs_in.append(pl.BlockSpec((None, ws, cols), lambda c, r: (j, c * nr + r, 0)))
        cast_specs_out.append(pl.BlockSpec((ws, cols), lambda c, r: (c * nr + r, 0)))
        cast_shapes.append(jax.ShapeDtypeStruct((rows, cols), BF16))
    outs = pl.pallas_call(
        functools.partial(_in_proj_kernel, norm=h.dtype == F32, ncast=len(later_weights)),
        grid=(n // tn, nr),
        in_specs=[pl.BlockSpec((tm, k), lambda c, r: (r, 0)),
                  pl.BlockSpec((None, 1, k), lambda c, r: (layer, 0, 0)),
                  pl.BlockSpec((None, k, tn), lambda c, r: (j, 0, c))] + cast_specs_in,
        out_specs=tuple([pl.BlockSpec((tm, tn), lambda c, r: (r, c))] + cast_specs_out),
        out_shape=tuple([jax.ShapeDtypeStruct((t, n), BF16)] + cast_shapes),
        scratch_shapes=[pltpu.VMEM((k, tn), BF16)],
        compiler_params=_cparams(),
        name="in_proj",
    )(h, gains, w_all, *later_weights)
    return outs[0], outs[1:]


def _pool_kernel(u_ref, wrap_ref, z_ref, wg_ref, sc_ref, o_ref, wb_ref, hist_ref, p_ref,
                 *, rows, cg):
    g = pl.program_id(0)
    b = pl.program_id(1)
    i = pl.program_id(2)

    @pl.when((i == 0) & (b == 0))
    def _():
        wb_ref[...] = wg_ref[...].astype(BF16)

    for gi, w in enumerate(POOL_WINDOWS):
        @pl.when(g == gi)
        def _(w=w):
            _pool_step(u_ref, wrap_ref, z_ref, sc_ref, o_ref, wb_ref, hist_ref, p_ref, i,
                       w=w, rows=rows, cg=cg)


def _pool_step(u_ref, wrap_ref, z_ref, sc_ref, o_ref, wb_ref, hist_ref, p_ref, i, *, w, rows, cg):
    @pl.when(i == 0)
    def _():
        seg0 = lax.broadcasted_iota(jnp.int32, (rows, 1), 0) == 0
        for m in range(1, w):
            blk = wrap_ref[CHUNK - m].astype(F32)
            hist_ref[m - 1] = jnp.where(seg0, 0.0, pltpu.roll(blk, 1, axis=0))

    def strip(rs, cs, first_tile):
        cur = [u_ref[t, rs, cs].astype(F32) for t in range(CHUNK)]
        s = {t: cur[t] for t in range(CHUNK)}
        for m in range(1, w):
            s[-m] = hist_ref[m - 1, rs, cs]
        lo, k = -(w - 1), 1
        while k < w:
            s = {t: s[t] + s[t - k] for t in range(lo + k, CHUNK)}
            lo += k
            k *= 2
        if first_tile:
            seq_start = (lax.broadcasted_iota(jnp.int32, (PACK, 1), 0) == 0) & (i == 0)
        for t in range(CHUNK):
            inv = 1.0 / w
            if first_tile:
                inv = jnp.where(seq_start, 1.0 / min(t + 1, w), inv)
            p_ref[t, rs, cs] = (s[t] * inv - cur[t]).astype(BF16)

    for r in range(rows // PACK):
        rs = slice(r * PACK, (r + 1) * PACK)
        for c in range(cg // LANES):
            strip(rs, slice(c * LANES, (c + 1) * LANES), r == 0)
        p = p_ref[:, rs, :].reshape(CHUNK * PACK, cg)
        m = jnp.dot(p, wb_ref[...], preferred_element_type=F32)
        zz = z_ref[:, rs, :].reshape(CHUNK * PACK, cg).astype(F32)
        y = m * sc_ref[...] * _silu(zz)
        o_ref[:, rs, :] = y.astype(o_ref.dtype).reshape(CHUNK, PACK, cg)

    for m in range(1, w):
        hist_ref[m - 1] = u_ref[CHUNK - m].astype(F32)


def _pool_mixer(uz, w_grp_all, scale_all, j, nb, ni):
    t, e2 = uz.shape
    e = e2 // 2
    ng, cg = w_grp_all.shape[1], w_grp_all.shape[2]
    assert ng == len(POOL_WINDOWS)
    nc = t // CHUNK
    rows = nc // (ni * nb)
    uz3 = uz.reshape(CHUNK, nc, e2)
    blk = (CHUNK, rows, cg)
    out = pl.pallas_call(
        functools.partial(_pool_kernel, rows=rows, cg=cg),
        grid=(ng, nb, ni),
        in_specs=[pl.BlockSpec(blk, lambda g, b, i: (0, i * nb + b, g)),
                  pl.BlockSpec(blk, lambda g, b, i: (0, (ni - 1) * nb + b, g)),
                  pl.BlockSpec(blk, lambda g, b, i: (0, i * nb + b, ng + g)),
                  pl.BlockSpec((None, None, cg, cg), lambda g, b, i: (j, g, 0, 0)),
                  pl.BlockSpec((None, 1, cg), lambda g, b, i: (j, 0, g))],
        out_specs=pl.BlockSpec(blk, lambda g, b, i: (0, i * nb + b, g)),
        out_shape=jax.ShapeDtypeStruct((CHUNK, nc, e), BF16),
        scratch_shapes=[pltpu.VMEM((cg, cg), BF16),
                        pltpu.VMEM((max(POOL_WINDOWS) - 1, rows, cg), F32),
                        pltpu.VMEM((CHUNK, rows, cg), BF16)],
        compiler_params=_cparams(),
        name="pool_mixer",
    )(uz3, uz3, uz3, w_grp_all, scale_all)
    return out.reshape(t, e)


def _causal_mask():
    n = CHUNK * SSM_P
    r_tau = lax.broadcasted_iota(jnp.int32, (n, n), 0) // SSM_P
    c_tau = lax.broadcasted_iota(jnp.int32, (n, n), 1) // SSM_P
    return jnp.where(r_tau >= c_tau, 1.0, 0.0).astype(BF16)


def _gen_group_weights(tab, tbc, mask):
    def re(k):
        return jnp.broadcast_to(tab[CHUNK - k:CHUNK - k + 1, :], (SSM_P, 2 * SSM_N))

    def im(k):
        return jnp.broadcast_to(tab[3 * CHUNK - k:3 * CHUNK - k + 1, :], (SSM_P, 2 * SSM_N))

    low = lax.broadcasted_iota(jnp.int32, (SSM_P, 2 * SSM_N), 1) < SSM_N
    bp, bq = tbc[_T_BP:_T_BP + SSM_P], tbc[_T_BQ:_T_BQ + SSM_P]
    cr, ci = tbc[_T_CR:_T_CR + SSM_P], tbc[_T_CI:_T_CI + SSM_P]
    bqs = jnp.where(low, -bq, bq)
    ca = jnp.where(low, cr, -ci)
    cb_ = jnp.where(low, -ci, -cr)
    wb = jnp.concatenate([bp * re(CHUNK - 1 - t) + bqs * im(CHUNK - 1 - t) for t in range(CHUNK)], axis=0)
    rn = jnp.concatenate([bp * re(-t) + bqs * im(-t) for t in range(CHUNK)], axis=0)
    cb = [ca * re(d) + cb_ * im(d) for d in range(CHUNK + 1)]
    lm = jnp.concatenate(cb[:CHUNK], axis=0)
    wc = jnp.concatenate(cb[1:], axis=0)
    lm_h, rn_h = lm.astype(BF16), rn.astype(BF16)
    lm_l = (lm - lm_h.astype(F32)).astype(BF16)
    rn_l = (rn - rn_h.astype(F32)).astype(BF16)
    kt = lax.dot_general(jnp.concatenate([lm_h, lm_h, lm_l], axis=1),
                         jnp.concatenate([rn_h, rn_l, rn_h], axis=1),
                         (((1,), (1,)), ((), ())), preferred_element_type=F32)
    return wb.T.astype(BF16), wc.astype(BF16), kt.astype(BF16) * mask


def _ssm_kernel(u_ref, tab_ref, tbc_ref, co_ref, c1_ref, c2_ref, d_ref, o_ref, xt_ref, ot_ref,
                kw_ref, st_ref, a_ref, ep_ref, eq_ref, hp_ref, mask_ref, *, gb, nc, ni, segs):
    n = SSM_N
    kk = CHUNK * SSM_P
    gl = LANES

    @pl.when(pl.program_id(0) == 0)
    def _():
        mask_ref[...] = _causal_mask()

    for t in range(CHUNK):
        xt_ref[t] = u_ref[t].T

    def cmul(ar, ai, xr, xi):
        return ar * xr - ai * xi, ar * xi + ai * xr

    def group_rows(j):
        return pl.ds(pl.multiple_of(j * SSM_P, SSM_P), SSM_P)

    def scan_rows(j):
        return pl.ds(pl.multiple_of(j * gl, gl), gl)

    def tiles(st):
        return ([st[:n, i * LANES:(i + 1) * LANES] for i in range(ni)],
                [st[n:, i * LANES:(i + 1) * LANES] for i in range(ni)])

    def head(j, carry):
        ut = xt_ref[:, group_rows(j), :].reshape(kk, nc)
        wbt, wct, ktt = _gen_group_weights(tab_ref[j], tbc_ref[j], mask_ref[...])
        kw_ref[j, :, :kk] = ktt
        kw_ref[j, :, kk:] = wct
        st = jnp.dot(wbt, ut, preferred_element_type=F32)
        st_ref[j] = st
        s_r, s_i = tiles(st)
        co = co_ref[j]
        ar = jnp.broadcast_to(co[:, 0:1], (n, LANES))
        ai = jnp.broadcast_to(co[:, 1:2], (n, LANES))
        a_ref[j] = jnp.concatenate([ar, ai], axis=0)
        er, ei = s_r[0], s_i[0]
        for i in range(1, ni):
            mr, mi = cmul(ar, ai, er, ei)
            er, ei = mr + s_r[i], mi + s_i[i]
        ep_ref[scan_rows(j), :] = jnp.concatenate([er, ei], axis=0).T
        eq_ref[scan_rows(j), :] = jnp.concatenate([ei, er], axis=0).T
        return carry

    lax.fori_loop(0, gb, head, 0, unroll=4)

    c1, c2 = c1_ref[...], c2_ref[...]
    pairs = gb * (gl // segs)

    def seg_step(s, carry):
        p, q = carry
        at = pl.ds(s, pairs, stride=segs)
        hp_ref[at, :] = p
        return (c1 * p + c2 * q + ep_ref[at, :], c1 * q - c2 * p + eq_ref[at, :])

    zero = jnp.zeros((pairs, 2 * n), F32)
    lax.fori_loop(0, segs, seg_step, (zero, zero), unroll=8)

    def tail(j, carry):
        rows = group_rows(j)
        ut = xt_ref[:, rows, :].reshape(kk, nc)
        s_r, s_i = tiles(st_ref[j])
        a = a_ref[j]
        ar, ai = a[:n], a[n:]
        e = hp_ref[scan_rows(j), :].T
        hr, hi = e[:n], e[n:]
        h_tiles = []
        for i in range(ni):
            h_tiles.append(jnp.concatenate([hr, hi], axis=0).astype(BF16))
            if i + 1 < ni:
                mr, mi = cmul(ar, ai, hr, hi)
                hr, hi = mr + s_r[i], mi + s_i[i]
        ht = jnp.concatenate(h_tiles, axis=1)
        yt = jnp.dot(kw_ref[j], jnp.concatenate([ut, ht], axis=0), preferred_element_type=F32)
        skip = jnp.concatenate([d_ref[j]] * ni, axis=1)
        y3 = yt.reshape(CHUNK, SSM_P, nc) + skip[None] * ut.astype(F32).reshape(CHUNK, SSM_P, nc)
        ot_ref[:, rows, :] = _gelu_tanh(y3)
        return carry

    lax.fori_loop(0, gb, tail, 0, unroll=4)

    for t in range(CHUNK):
        o_ref[t] = ot_ref[t].T.astype(o_ref.dtype)


def _ssm_mixer(uz, tables, j, nb, ni, gb=8):
    tab, tbc, coef, c1s, c2s, d_tab = tables
    t, e2 = uz.shape
    e = e2 // 2
    g = e // SSM_P
    nc = t // CHUNK
    segs = LANES // nb
    uz3 = uz.reshape(CHUNK, nc, e2)
    wl = gb * SSM_P
    kern = functools.partial(_ssm_kernel, gb=gb, nc=nc, ni=ni, segs=segs)
    out = pl.pallas_call(
        kern,
        grid=(g // gb,),
        in_specs=[pl.BlockSpec((CHUNK, nc, wl), lambda i: (0, 0, i)),
                  pl.BlockSpec((None, gb, _T_POW_ROWS, 2 * SSM_N), lambda i: (j, i, 0, 0)),
                  pl.BlockSpec((None, gb, _T_BC_ROWS, 2 * SSM_N), lambda i: (j, i, 0, 0)),
                  pl.BlockSpec((None, gb, SSM_N, 2), lambda i: (j, i, 0, 0)),
                  pl.BlockSpec((None, gb * nb, 2 * SSM_N), lambda i: (j, i, 0)),
                  pl.BlockSpec((None, gb * nb, 2 * SSM_N), lambda i: (j, i, 0)),
                  pl.BlockSpec((None, gb, SSM_P, LANES), lambda i: (j, i, 0, 0))],
        out_specs=pl.BlockSpec((CHUNK, nc, wl), lambda i: (0, 0, i)),
        out_shape=jax.ShapeDtypeStruct((CHUNK, nc, e), BF16),
        scratch_shapes=[pltpu.VMEM((CHUNK, wl, nc), BF16),
                        pltpu.VMEM((CHUNK, wl, nc), F32),
                        pltpu.VMEM((gb, CHUNK * SSM_P, CHUNK * SSM_P + 2 * SSM_N), BF16),
                        pltpu.VMEM((gb, 2 * SSM_N, nc), F32),
                        pltpu.VMEM((gb, 2 * SSM_N, LANES), F32),
                        pltpu.VMEM((gb * LANES, 2 * SSM_N), F32),
                        pltpu.VMEM((gb * LANES, 2 * SSM_N), F32),
                        pltpu.VMEM((gb * LANES, 2 * SSM_N), F32),
                        pltpu.VMEM((CHUNK * SSM_P, CHUNK * SSM_P), BF16)],
        compiler_params=_cparams(),
        name="ssm_mixer",
    )(uz3, tab, tbc, coef, c1s, c2s, d_tab)
    return out.reshape(t, e)


def _ssm_tables(a_re, a_im, log_dt, b_re, b_im, c_re, c_im, d_skip, seg_tokens, nb):
    dt = jnp.exp(log_dt)[..., None]
    la, th = a_re * dt, a_im * dt
    mag = jnp.exp(la)
    abr = mag * jnp.cos(th)
    abi = mag * jnp.sin(th)
    den = a_re * a_re + a_im * a_im
    nr = abr - 1.0
    fr = (nr * a_re + abi * a_im) / den
    fi = (abi * a_re - nr * a_im) / den
    bbr = fr[..., None] * b_re - fi[..., None] * b_im
    bbi = fr[..., None] * b_im + fi[..., None] * b_re

    def cat(x, y):
        return jnp.concatenate([x, y], axis=-1)

    def power(k, la_, th_):
        m = jnp.exp(k * la_)
        return m * jnp.cos(k * th_), m * jnp.sin(k * th_)

    k = (CHUNK - np.arange(2 * CHUNK)).astype(np.float32)
    pr, pi = power(k[:, None], cat(la, la)[:, :, None, :], cat(th, th)[:, :, None, :])
    tab = jnp.concatenate([pr, pi], axis=2)

    bt_r, bt_i = jnp.swapaxes(bbr, -1, -2), jnp.swapaxes(bbi, -1, -2)
    tbc = jnp.concatenate([cat(bt_r, bt_i), cat(bt_i, bt_r),
                           cat(c_re, c_re), cat(c_im, c_im)], axis=2)

    coef = jnp.stack(power(float(CHUNK), la, th), axis=-1)
    sr, si = power(float(seg_tokens), la, th)
    c1s = jnp.repeat(cat(sr, sr), nb, axis=1)
    c2s = jnp.repeat(cat(-si, si), nb, axis=1)
    d_tab = jnp.broadcast_to(d_skip.reshape(d_skip.shape[0], -1, SSM_P, 1),
                             (d_skip.shape[0], d_skip.shape[1] // SSM_P, SSM_P, LANES))
    return tab, tbc, coef, c1s, c2s, d_tab


def _glu_kernel(a_ref, w_ref, b_ref, g_ref, z_ref, o_ref):
    nsub = o_ref.shape[1] // GLU_SUB

    def matmul(s):
        return jnp.dot(a_ref[...], w_ref[:, s * GLU_SUB:(s + 1) * GLU_SUB],
                       preferred_element_type=F32)

    acc = matmul(0)
    for s in range(nsub):
        cs = slice(s * GLU_SUB, (s + 1) * GLU_SUB)
        nxt = matmul(s + 1) if s + 1 < nsub else None
        gg = g_ref[:, cs].astype(F32)
        zz = z_ref[:, cs].astype(F32)
        den = (1.0 + jnp.exp2(-LOG2E * (acc + b_ref[:, cs]))) * (1.0 + jnp.exp2(-LOG2E * zz))
        o_ref[:, cs] = (gg * zz / den).astype(o_ref.dtype)
        acc = nxt


def _glu(gact, w, b_all, uz, j, tm=512, tn=1024):
    t, e = gact.shape
    zoff = e // tn
    return pl.pallas_call(
        _glu_kernel,
        grid=(e // tn, t // tm),
        in_specs=[pl.BlockSpec((tm, e), lambda c, r: (r, 0)),
                  pl.BlockSpec((e, tn), lambda c, r: (0, c)),
                  pl.BlockSpec((None, 1, tn), lambda c, r: (j, 0, c)),
                  pl.BlockSpec((tm, tn), lambda c, r: (r, c)),
                  pl.BlockSpec((tm, tn), lambda c, r: (r, c + zoff))],
        out_specs=pl.BlockSpec((tm, tn), lambda c, r: (r, c)),
        out_shape=jax.ShapeDtypeStruct((t, e), BF16),
        compiler_params=_cparams(),
        name="glu",
    )(gact, w, b_all, gact, uz)


def _out_proj_kernel(y_ref, w_ref, x_ref, g_ref, *o_refs, last):
    xn = x_ref[...] + jnp.dot(y_ref[...], w_ref[...], preferred_element_type=F32)
    inv = lax.rsqrt(jnp.mean(xn * xn, axis=-1, keepdims=True) + NORM_EPS)
    hn = xn * inv * g_ref[...]
    if last:
        o_refs[0][...] = hn
    else:
        o_refs[0][...] = xn
        o_refs[1][...] = hn.astype(o_refs[1].dtype)


def _out_proj(y, w, x, gains, layer, last, tm=256):
    t, e = y.shape
    d = w.shape[1]
    row = pl.BlockSpec((tm, d), lambda i: (i, 0))
    if last:
        out_shape = jax.ShapeDtypeStruct((t, d), F32)
        out_specs = row
    else:
        out_shape = (jax.ShapeDtypeStruct((t, d), F32), jax.ShapeDtypeStruct((t, d), BF16))
        out_specs = (row, row)
    return pl.pallas_call(
        functools.partial(_out_proj_kernel, last=last),
        grid=(t // tm,),
        in_specs=[pl.BlockSpec((tm, e), lambda i: (i, 0)),
                  pl.BlockSpec((e, d), lambda i: (0, 0)),
                  row,
                  pl.BlockSpec((None, 1, d), lambda i: (layer, 0, 0))],
        out_specs=out_specs,
        out_shape=out_shape,
        compiler_params=_cparams(),
        name="out_proj",
    )(y, w, x, gains)


def kernel(x, norm_g, final_norm_g, pool_w_in, pool_w_grp, pool_scale, pool_w_out, ssm_w_in, ssm_a_re, ssm_a_im, ssm_log_dt, ssm_b_re, ssm_b_im, ssm_c_re, ssm_c_im, ssm_d, ssm_w_glu, ssm_b_glu, ssm_w_out):
    nb, seq_len, d = x.shape
    depth = norm_g.shape[0]
    segs = LANES // nb
    ni = seq_len // (CHUNK * segs)
    assert nb * segs == LANES and ni * segs * CHUNK == seq_len
    t = nb * seq_len

    gains = jnp.concatenate([norm_g, final_norm_g[None]], axis=0)[:, None, :]
    pool_scale3 = pool_scale[:, None, :]
    ssm_b_glu3 = ssm_b_glu[:, None, :]
    ssm_tables = _ssm_tables(ssm_a_re, ssm_a_im, ssm_log_dt, ssm_b_re, ssm_b_im,
                             ssm_c_re, ssm_c_im, ssm_d, ni * CHUNK, nb)

    xs = x.reshape(nb, segs, ni, CHUNK, d).transpose(3, 2, 0, 1, 4).reshape(t, d)
    h = xs
    for layer in range(depth):
        j = layer // 2
        if layer % 2 == 0:
            uz, (w_out,) = _in_proj(h, gains, layer, pool_w_in, j, [pool_w_out])
            y = _pool_mixer(uz, pool_w_grp, pool_scale3, j, nb, ni)
        else:
            uz, (w_out, w_glu) = _in_proj(h, gains, layer, ssm_w_in, j, [ssm_w_out, ssm_w_glu])
            gact = _ssm_mixer(uz, ssm_tables, j, nb, ni)
            y = _glu(gact, w_glu, ssm_b_glu3, uz, j)
        last = layer == depth - 1
        res = _out_proj(y, w_out, xs, gains, layer + 1, last)
        if last:
            return res.reshape(CHUNK, ni, nb, segs, d).transpose(2, 3, 1, 0, 4).reshape(nb, seq_len, d)
        xs, h = res
```

```python
import functools
import math

import jax
import jax.numpy as jnp
import numpy as np
from jax import lax
from jax.experimental import pallas as pl
from jax.experimental.pallas import tpu as pltpu

F32 = jnp.float32
BF16 = jnp.bfloat16

POOL_WINDOWS = (2, 4, 8, 16)
SSM_P = 16
SSM_N = 64
CHUNK = 16
LANES = 128
GLU_SUB = 256
PACK = 16
LOG2E = math.log2(math.e)
NORM_EPS = 1e-6
VMEM_LIMIT = 56 * 1024 * 1024

_T_POW_ROWS = 4 * CHUNK


def _cparams():
    return pltpu.CompilerParams(vmem_limit_bytes=VMEM_LIMIT)


def _silu(z):
    return z * jax.nn.sigmoid(z)


def _gelu_tanh(y):
    a = -2.0 * math.sqrt(2.0 / math.pi) * math.log2(math.e)
    return y / (1.0 + jnp.exp2(y * (a + (a * 0.044715) * (y * y))))


def _in_proj_kernel(*refs, norm, ncast):
    a_ref, g_ref, w_ref = refs[:3]
    cast_in, o_ref = refs[3:3 + ncast], refs[3 + ncast]
    cast_out, wb_ref = refs[4 + ncast:4 + 2 * ncast], refs[4 + 2 * ncast]

    @pl.when(pl.program_id(1) == 0)
    def _():
        wb_ref[...] = w_ref[...].astype(BF16)

    a = a_ref[...]
    if norm:
        inv = lax.rsqrt(jnp.mean(a * a, axis=-1, keepdims=True) + NORM_EPS)
        a = (a * inv * g_ref[...]).astype(BF16)
    o_ref[...] = jnp.dot(a, wb_ref[...], preferred_element_type=F32).astype(o_ref.dtype)
    for src, dst in zip(cast_in, cast_out):
        dst[...] = src[...].astype(BF16)


def _in_proj(h, gains, layer, w_all, j, later_weights, tm=1024, tn=1024):
    t, k = h.shape
    n = w_all.shape[2]
    nr = t // tm
    steps = (n // tn) * nr
    cast_specs_in, cast_specs_out, cast_shapes = [], [], []
    for w in later_weights:
        rows, cols = w.shape[1], w.shape[2]
        ws = rows // steps
        cast_specs_in.append(pl.BlockSpec((None, ws, cols), lambda c, r: (j, c * nr + r, 0)))
        cast_specs_out.append(pl.BlockSpec((ws, cols), lambda c, r: (c * nr + r, 0)))
        cast_shapes.append(jax.ShapeDtypeStruct((rows, cols), BF16))
    outs = pl.pallas_call(
        functools.partial(_in_proj_kernel, norm=h.dtype == F32, ncast=len(later_weights)),
        grid=(n // tn, nr),
        in_specs=[pl.BlockSpec((tm, k), lambda c, r: (r, 0)),
                  pl.BlockSpec((None, 1, k), lambda c, r: (layer, 0, 0)),
                  pl.BlockSpec((None, k, tn), lambda c, r: (j, 0, c))] + cast_specs_in,
        out_specs=tuple([pl.BlockSpec((tm, tn), lambda c, r: (r, c))] + cast_specs_out),
        out_shape=tuple([jax.ShapeDtypeStruct((t, n), BF16)] + cast_shapes),
        scratch_shapes=[pltpu.VMEM((k, tn), BF16)],
        compiler_params=_cparams(),
        name="in_proj",
    )(h, gains, w_all, *later_weights)
    return outs[0], outs[1:]


def _pool_kernel(u_ref, wrap_ref, z_ref, wg_ref, sc_ref, o_ref, wb_ref, hist_ref, p_ref,
                 *, rows, cg):
    g = pl.program_id(0)
    b = pl.program_id(1)
    i = pl.program_id(2)

    @pl.when((i == 0) & (b == 0))
    def _():
        wb_ref[...] = wg_ref[...].astype(BF16)

    for gi, w in enumerate(POOL_WINDOWS):
        @pl.when(g == gi)
        def _(w=w):
            _pool_step(u_ref, wrap_ref, z_ref, sc_ref, o_ref, wb_ref, hist_ref, p_ref, i,
                       w=w, rows=rows, cg=cg)


def _pool_step(u_ref, wrap_ref, z_ref, sc_ref, o_ref, wb_ref, hist_ref, p_ref, i, *, w, rows, cg):
    @pl.when(i == 0)
    def _():
        seg0 = lax.broadcasted_iota(jnp.int32, (rows, 1), 0) == 0
        for m in range(1, w):
            blk = wrap_ref[CHUNK - m].astype(F32)
            hist_ref[m - 1] = jnp.where(seg0, 0.0, pltpu.roll(blk, 1, axis=0))

    def strip(rs, cs, first_tile):
        cur = [u_ref[t, rs, cs].astype(F32) for t in range(CHUNK)]
        s = {t: cur[t] for t in range(CHUNK)}
        for m in range(1, w):
            s[-m] = hist_ref[m - 1, rs, cs]
        lo, k = -(w - 1), 1
        while k < w:
            s = {t: s[t] + s[t - k] for t in range(lo + k, CHUNK)}
            lo += k
            k *= 2
        if first_tile:
            seq_start = (lax.broadcasted_iota(jnp.int32, (PACK, 1), 0) == 0) & (i == 0)
        for t in range(CHUNK):
            inv = 1.0 / w
            if first_tile:
                inv = jnp.where(seq_start, 1.0 / min(t + 1, w), inv)
            p_ref[t, rs, cs] = (s[t] * inv - cur[t]).astype(BF16)

    for r in range(rows // PACK):
        rs = slice(r * PACK, (r + 1) * PACK)
        for c in range(cg // LANES):
            strip(rs, slice(c * LANES, (c + 1) * LANES), r == 0)
        p = p_ref[:, rs, :].reshape(CHUNK * PACK, cg)
        m = jnp.dot(p, wb_ref[...], preferred_element_type=F32)
        zz = z_ref[:, rs, :].reshape(CHUNK * PACK, cg).astype(F32)
        y = m * sc_ref[...] * _silu(zz)
        o_ref[:, rs, :] = y.astype(o_ref.dtype).reshape(CHUNK, PACK, cg)

    for m in range(1, w):
        hist_ref[m - 1] = u_ref[CHUNK - m].astype(F32)


def _pool_mixer(uz, w_grp_all, scale_all, j, nb, ni):
    t, e2 = uz.shape
    e = e2 // 2
    ng, cg = w_grp_all.shape[1], w_grp_all.shape[2]
    assert ng == len(POOL_WINDOWS)
    nc = t // CHUNK
    rows = nc // (ni * nb)
    uz3 = uz.reshape(CHUNK, nc, e2)
    blk = (CHUNK, rows, cg)
    out = pl.pallas_call(
        functools.partial(_pool_kernel, rows=rows, cg=cg),
        grid=(ng, nb, ni),
        in_specs=[pl.BlockSpec(blk, lambda g, b, i: (0, i * nb + b, g)),
                  pl.BlockSpec(blk, lambda g, b, i: (0, (ni - 1) * nb + b, g)),
                  pl.BlockSpec(blk, lambda g, b, i: (0, i * nb + b, ng + g)),
                  pl.BlockSpec((None, None, cg, cg), lambda g, b, i: (j, g, 0, 0)),
                  pl.BlockSpec((None, 1, cg), lambda g, b, i: (j, 0, g))],
        out_specs=pl.BlockSpec(blk, lambda g, b, i: (0, i * nb + b, g)),
        out_shape=jax.ShapeDtypeStruct((CHUNK, nc, e), BF16),
        scratch_shapes=[pltpu.VMEM((cg, cg), BF16),
                        pltpu.VMEM((max(POOL_WINDOWS) - 1, rows, cg), F32),
                        pltpu.VMEM((CHUNK, rows, cg), BF16)],
        compiler_params=_cparams(),
        name="pool_mixer",
    )(uz3, uz3, uz3, w_grp_all, scale_all)
    return out.reshape(t, e)


def _causal_mask():
    n = CHUNK * SSM_P
    r_tau = lax.broadcasted_iota(jnp.int32, (n, n), 0) // SSM_P
    c_tau = lax.broadcasted_iota(jnp.int32, (n, n), 1) // SSM_P
    return jnp.where(r_tau >= c_tau, 1.0, 0.0).astype(BF16)


def _gen_group_weights(tab, bp, bq, cr, ci, mask):
    def re(k):
        return jnp.broadcast_to(tab[CHUNK - k:CHUNK - k + 1, :], (SSM_P, 2 * SSM_N))

    def im(k):
        return jnp.broadcast_to(tab[3 * CHUNK - k:3 * CHUNK - k + 1, :], (SSM_P, 2 * SSM_N))

    low = lax.broadcasted_iota(jnp.int32, (SSM_P, 2 * SSM_N), 1) < SSM_N
    bqs = jnp.where(low, -bq, bq)
    ca = jnp.where(low, cr, -ci)
    cb_ = jnp.where(low, -ci, -cr)
    wb = jnp.concatenate([bp * re(CHUNK - 1 - t) + bqs * im(CHUNK - 1 - t) for t in range(CHUNK)], axis=0)
    rn = jnp.concatenate([bp * re(-t) + bqs * im(-t) for t in range(CHUNK)], axis=0)
    cb = [ca * re(d) + cb_ * im(d) for d in range(CHUNK + 1)]
    lm = jnp.concatenate(cb[:CHUNK], axis=0)
    wc = jnp.concatenate(cb[1:], axis=0)
    lm_h, rn_h = lm.astype(BF16), rn.astype(BF16)
    lm_l = (lm - lm_h.astype(F32)).astype(BF16)
    rn_l = (rn - rn_h.astype(F32)).astype(BF16)
    kt = lax.dot_general(jnp.concatenate([lm_h, lm_h, lm_l], axis=1),
                         jnp.concatenate([rn_h, rn_l, rn_h], axis=1),
                         (((1,), (1,)), ((), ())), preferred_element_type=F32)
    return wb.T.astype(BF16), wc.astype(BF16), kt.astype(BF16) * mask


def _ssm_kernel(u_ref, tab_ref, btr_ref, bti_ref, cre_ref, cim_ref, c1_ref, c2_ref, d_ref, o_ref,
                xt_ref, ot_ref, kw_ref, st_ref, a_ref, ep_ref, eq_ref, hp_ref, mask_ref,
                *, gb, nc, ni, segs):
    n = SSM_N
    kk = CHUNK * SSM_P
    nbat = LANES // segs

    @pl.when(pl.program_id(0) == 0)
    def _():
        mask_ref[...] = _causal_mask()

    for t in range(CHUNK):
        xt_ref[t] = u_ref[t].T

    def cmul(ar, ai, xr, xi):
        return ar * xr - ai * xi, ar * xi + ai * xr

    def group_rows(j):
        return pl.ds(pl.multiple_of(j * SSM_P, SSM_P), SSM_P)

    def scan_rows(j, b):
        return pl.ds(pl.multiple_of((b * gb + j) * segs, segs), segs)

    def lanes2(x, y):
        return jnp.concatenate([x, y], axis=1)

    def tiles(st):
        return ([st[:n, i * LANES:(i + 1) * LANES] for i in range(ni)],
                [st[n:, i * LANES:(i + 1) * LANES] for i in range(ni)])

    def head(j, carry):
        ut = xt_ref[:, group_rows(j), :].reshape(kk, nc)
        t64 = tab_ref[j]
        tab = lanes2(t64, t64)
        btr, bti, cre, cim = btr_ref[j], bti_ref[j], cre_ref[j], cim_ref[j]
        wbt, wct, ktt = _gen_group_weights(tab, lanes2(btr, bti), lanes2(bti, btr),
                                           lanes2(cre, cre), lanes2(cim, cim), mask_ref[...])
        kw_ref[j, :, :kk] = ktt
        kw_ref[j, :, kk:] = wct
        st = jnp.dot(wbt, ut, preferred_element_type=F32)
        st_ref[j] = st
        s_r, s_i = tiles(st)
        rows8 = jnp.concatenate([tab[0:8], tab[2 * CHUNK:2 * CHUNK + 8]], axis=0)
        cols = jnp.concatenate([rows8] * (LANES // 16), axis=0).T
        ar = jnp.broadcast_to(cols[:n, 0:1], (n, LANES))
        ai = jnp.broadcast_to(cols[:n, 8:9], (n, LANES))
        a_ref[j] = jnp.concatenate([ar, ai], axis=0)
        er, ei = s_r[0], s_i[0]
        for i in range(1, ni):
            mr, mi = cmul(ar, ai, er, ei)
            er, ei = mr + s_r[i], mi + s_i[i]
        ep = jnp.concatenate([er, ei], axis=0).T
        eq = jnp.concatenate([ei, er], axis=0).T
        for b in range(nbat):
            ep_ref[scan_rows(j, b), :] = ep[b * segs:(b + 1) * segs]
            eq_ref[scan_rows(j, b), :] = eq[b * segs:(b + 1) * segs]
        return carry

    lax.fori_loop(0, gb, head, 0, unroll=4)

    c1 = jnp.concatenate([c1_ref[...]] * nbat, axis=0)
    c2 = jnp.concatenate([c2_ref[...]] * nbat, axis=0)
    pairs = gb * nbat

    def seg_step(s, carry):
        p, q = carry
        at = pl.ds(s, pairs, stride=segs)
        hp_ref[at, :] = p
        return (c1 * p + c2 * q + ep_ref[at, :], c1 * q - c2 * p + eq_ref[at, :])

    zero = jnp.zeros((pairs, 2 * n), F32)
    lax.fori_loop(0, segs, seg_step, (zero, zero), unroll=8)

    def tail(j, carry):
        rows = group_rows(j)
        ut = xt_ref[:, rows, :].reshape(kk, nc)
        s_r, s_i = tiles(st_ref[j])
        a = a_ref[j]
        ar, ai = a[:n], a[n:]
        e = jnp.concatenate([hp_ref[scan_rows(j, b), :] for b in range(nbat)], axis=0).T
        hr, hi = e[:n], e[n:]
        h_tiles = []
        for i in range(ni):
            h_tiles.append(jnp.concatenate([hr, hi], axis=0).astype(BF16))
            if i + 1 < ni:
                mr, mi = cmul(ar, ai, hr, hi)
                hr, hi = mr + s_r[i], mi + s_i[i]
        ht = jnp.concatenate(h_tiles, axis=1)
        yt = jnp.dot(kw_ref[j], jnp.concatenate([ut, ht], axis=0), preferred_element_type=F32)
        skip = jnp.concatenate([d_ref[j]] * ni, axis=1)
        y3 = yt.reshape(CHUNK, SSM_P, nc) + skip[None] * ut.astype(F32).reshape(CHUNK, SSM_P, nc)
        ot_ref[:, rows, :] = _gelu_tanh(y3)
        return carry

    lax.fori_loop(0, gb, tail, 0, unroll=4)

    for t in range(CHUNK):
        o_ref[t] = ot_ref[t].T.astype(o_ref.dtype)


def _ssm_mixer(uz, tables, j, nb, ni, gb=8):
    tab, bt_r, bt_i, c_re, c_im, c1, c2, d_tab = tables
    small = pl.BlockSpec((None, gb, SSM_P, SSM_N), lambda i: (j, i, 0, 0))
    row = pl.BlockSpec((None, gb, 2 * SSM_N), lambda i: (j, i, 0))
    t, e2 = uz.shape
    e = e2 // 2
    g = e // SSM_P
    nc = t // CHUNK
    segs = LANES // nb
    uz3 = uz.reshape(CHUNK, nc, e2)
    wl = gb * SSM_P
    kern = functools.partial(_ssm_kernel, gb=gb, nc=nc, ni=ni, segs=segs)
    out = pl.pallas_call(
        kern,
        grid=(g // gb,),
        in_specs=[pl.BlockSpec((CHUNK, nc, wl), lambda i: (0, 0, i)),
                  pl.BlockSpec((None, gb, _T_POW_ROWS, SSM_N), lambda i: (j, i, 0, 0)),
                  small, small, small, small, row, row,
                  pl.BlockSpec((None, gb, SSM_P, LANES), lambda i: (j, i, 0, 0))],
        out_specs=pl.BlockSpec((CHUNK, nc, wl), lambda i: (0, 0, i)),
        out_shape=jax.ShapeDtypeStruct((CHUNK, nc, e), BF16),
        scratch_shapes=[pltpu.VMEM((CHUNK, wl, nc), BF16),
                        pltpu.VMEM((CHUNK, wl, nc), F32),
                        pltpu.VMEM((gb, CHUNK * SSM_P, CHUNK * SSM_P + 2 * SSM_N), BF16),
                        pltpu.VMEM((gb, 2 * SSM_N, nc), F32),
                        pltpu.VMEM((gb, 2 * SSM_N, LANES), F32),
                        pltpu.VMEM((gb * LANES, 2 * SSM_N), F32),
                        pltpu.VMEM((gb * LANES, 2 * SSM_N), F32),
                        pltpu.VMEM((gb * LANES, 2 * SSM_N), F32),
                        pltpu.VMEM((CHUNK * SSM_P, CHUNK * SSM_P), BF16)],
        compiler_params=_cparams(),
        name="ssm_mixer",
    )(uz3, tab, bt_r, bt_i, c_re, c_im, c1, c2, d_tab)
    return out.reshape(t, e)


def _ssm_tables(a_re, a_im, log_dt, b_re, b_im, c_re, c_im, d_skip, seg_tokens):
    dt = jnp.exp(log_dt)[..., None]
    la, th = a_re * dt, a_im * dt
    mag = jnp.exp(la)
    abr = mag * jnp.cos(th)
    abi = mag * jnp.sin(th)
    den = a_re * a_re + a_im * a_im
    nr = abr - 1.0
    fr = (nr * a_re + abi * a_im) / den
    fi = (abi * a_re - nr * a_im) / den
    bbr = fr[..., None] * b_re - fi[..., None] * b_im
    bbi = fr[..., None] * b_im + fi[..., None] * b_re

    def cat(x, y):
        return jnp.concatenate([x, y], axis=-1)

    def power(k, la_, th_):
        m = jnp.exp(k * la_)
        return m * jnp.cos(k * th_), m * jnp.sin(k * th_)

    k = (CHUNK - np.arange(2 * CHUNK)).astype(np.float32)
    pr, pi = power(k[:, None], la[:, :, None, :], th[:, :, None, :])
    tab = jnp.concatenate([pr, pi], axis=2)
    bt_r, bt_i = jnp.swapaxes(bbr, -1, -2), jnp.swapaxes(bbi, -1, -2)

    sr, si = power(float(seg_tokens), la, th)
    d_tab = jnp.broadcast_to(d_skip.reshape(d_skip.shape[0], -1, SSM_P, 1),
                             (d_skip.shape[0], d_skip.shape[1] // SSM_P, SSM_P, LANES))
    return tab, bt_r, bt_i, c_re, c_im, cat(sr, sr), cat(-si, si), d_tab


def _glu_kernel(a_ref, w_ref, b_ref, g_ref, z_ref, o_ref):
    nsub = o_ref.shape[1] // GLU_SUB

    def matmul(s):
        return jnp.dot(a_ref[...], w_ref[:, s * GLU_SUB:(s + 1) * GLU_SUB],
                       preferred_element_type=F32)

    acc = matmul(0)
    for s in range(nsub):
        cs = slice(s * GLU_SUB, (s + 1) * GLU_SUB)
        nxt = matmul(s + 1) if s + 1 < nsub else None
        gg = g_ref[:, cs].astype(F32)
        zz = z_ref[:, cs].astype(F32)
        den = (1.0 + jnp.exp2(-LOG2E * (acc + b_ref[:, cs]))) * (1.0 + jnp.exp2(-LOG2E * zz))
        o_ref[:, cs] = (gg * zz / den).astype(o_ref.dtype)
        acc = nxt


def _glu(gact, w, b_all, uz, j, tm=512, tn=1024):
    t, e = gact.shape
    zoff = e // tn
    return pl.pallas_call(
        _glu_kernel,
        grid=(e // tn, t // tm),
        in_specs=[pl.BlockSpec((tm, e), lambda c, r: (r, 0)),
                  pl.BlockSpec((e, tn), lambda c, r: (0, c)),
                  pl.BlockSpec((None, 1, tn), lambda c, r: (j, 0, c)),
                  pl.BlockSpec((tm, tn), lambda c, r: (r, c)),
                  pl.BlockSpec((tm, tn), lambda c, r: (r, c + zoff))],
        out_specs=pl.BlockSpec((tm, tn), lambda c, r: (r, c)),
        out_shape=jax.ShapeDtypeStruct((t, e), BF16),
        compiler_params=_cparams(),
        name="glu",
    )(gact, w, b_all, gact, uz)


def _out_proj_kernel(y_ref, w_ref, x_ref, g_ref, *o_refs, last):
    xn = x_ref[...] + jnp.dot(y_ref[...], w_ref[...], preferred_element_type=F32)
    inv = lax.rsqrt(jnp.mean(xn * xn, axis=-1, keepdims=True) + NORM_EPS)
    hn = xn * inv * g_ref[...]
    if last:
        o_refs[0][...] = hn
    else:
        o_refs[0][...] = xn
        o_refs[1][...] = hn.astype(o_refs[1].dtype)


def _out_proj(y, w, x, gains, layer, last, tm=256):
    t, e = y.shape
    d = w.shape[1]
    row = pl.BlockSpec((tm, d), lambda i: (i, 0))
    if last:
        out_shape = jax.ShapeDtypeStruct((t, d), F32)
        out_specs = row
    else:
        out_shape = (jax.ShapeDtypeStruct((t, d), F32), jax.ShapeDtypeStruct((t, d), BF16))
        out_specs = (row, row)
    return pl.pallas_call(
        functools.partial(_out_proj_kernel, last=last),
        grid=(t // tm,),
        in_specs=[pl.BlockSpec((tm, e), lambda i: (i, 0)),
                  pl.BlockSpec((e, d), lambda i: (0, 0)),
                  row,
                  pl.BlockSpec((None, 1, d), lambda i: (layer, 0, 0))],
        out_specs=out_specs,
        out_shape=out_shape,
        compiler_params=_cparams(),
        name="out_proj",
    )(y, w, x, gains)


def kernel(x, norm_g, final_norm_g, pool_w_in, pool_w_grp, pool_scale, pool_w_out, ssm_w_in, ssm_a_re, ssm_a_im, ssm_log_dt, ssm_b_re, ssm_b_im, ssm_c_re, ssm_c_im, ssm_d, ssm_w_glu, ssm_b_glu, ssm_w_out):
    nb, seq_len, d = x.shape
    depth = norm_g.shape[0]
    segs = LANES // nb
    ni = seq_len // (CHUNK * segs)
    assert nb * segs == LANES and ni * segs * CHUNK == seq_len
    t = nb * seq_len

    gains = jnp.concatenate([norm_g, final_norm_g[None]], axis=0)[:, None, :]
    pool_scale3 = pool_scale[:, None, :]
    ssm_b_glu3 = ssm_b_glu[:, None, :]
    ssm_tables = _ssm_tables(ssm_a_re, ssm_a_im, ssm_log_dt, ssm_b_re, ssm_b_im,
                             ssm_c_re, ssm_c_im, ssm_d, ni * CHUNK)

    xs = x.reshape(nb, segs, ni, CHUNK, d).transpose(3, 2, 0, 1, 4).reshape(t, d)
    h = xs
    for layer in range(depth):
        j = layer // 2
        if layer % 2 == 0:
            uz, (w_out,) = _in_proj(h, gains, layer, pool_w_in, j, [pool_w_out])
            y = _pool_mixer(uz, pool_w_grp, pool_scale3, j, nb, ni)
        else:
            uz, (w_out, w_glu) = _in_proj(h, gains, layer, ssm_w_in, j, [ssm_w_out, ssm_w_glu])
            gact = _ssm_mixer(uz, ssm_tables, j, nb, ni)
            y = _glu(gact, w_glu, ssm_b_glu3, uz, j)
        last = layer == depth - 1
        res = _out_proj(y, w_out, xs, gains, layer + 1, last)
        if last:
            return res.reshape(CHUNK, ni, nb, segs, d).transpose(2, 3, 1, 0, 4).reshape(nb, seq_len, d)
        xs, h = res
```

```python
import functools
import math

import jax
import jax.numpy as jnp
import numpy as np
from jax import lax
from jax.experimental import pallas as pl
from jax.experimental.pallas import tpu as pltpu

F32 = jnp.float32
BF16 = jnp.bfloat16

POOL_WINDOWS = (2, 4, 8, 16)
SSM_P = 16
SSM_N = 64
CHUNK = 16
LANES = 128
GLU_SUB = 256
PACK = 16
LOG2E = math.log2(math.e)
NORM_EPS = 1e-6
VMEM_LIMIT = 56 * 1024 * 1024

_T_POW_ROWS = 4 * CHUNK


def _cparams():
    return pltpu.CompilerParams(vmem_limit_bytes=VMEM_LIMIT)


def _silu(z):
    return z * jax.nn.sigmoid(z)


def _gelu_tanh(y):
    a = -2.0 * math.sqrt(2.0 / math.pi) * math.log2(math.e)
    return y / (1.0 + jnp.exp2(y * (a + (a * 0.044715) * (y * y))))


def _in_proj_kernel(*refs, norm, ncast):
    a_ref, g_ref, w_ref = refs[:3]
    cast_in, o_ref = refs[3:3 + ncast], refs[3 + ncast]
    cast_out, wb_ref = refs[4 + ncast:4 + 2 * ncast], refs[4 + 2 * ncast]

    @pl.when(pl.program_id(1) == 0)
    def _():
        wb_ref[...] = w_ref[...].astype(BF16)

    a = a_ref[...]
    if norm:
        inv = lax.rsqrt(jnp.mean(a * a, axis=-1, keepdims=True) + NORM_EPS)
        a = (a * inv * g_ref[...]).astype(BF16)
    o_ref[...] = jnp.dot(a, wb_ref[...], preferred_element_type=F32).astype(o_ref.dtype)
    for src, dst in zip(cast_in, cast_out):
        dst[...] = src[...].astype(BF16)


def _in_proj(h, gains, layer, w_all, j, later_weights, tm=1024, tn=1024):
    t, k = h.shape
    n = w_all.shape[2]
    nr = t // tm
    steps = (n // tn) * nr
    cast_specs_in, cast_specs_out, cast_shapes = [], [], []
    for w in later_weights:
        rows, cols = w.shape[1], w.shape[2]
        ws = rows // steps
        cast_specs_in.append(pl.BlockSpec((None, ws, cols), lambda c, r: (j, c * nr + r, 0)))
        cast_specs_out.append(pl.BlockSpec((ws, cols), lambda c, r: (c * nr + r, 0)))
        cast_shapes.append(jax.ShapeDtypeStruct((rows, cols), BF16))
    outs = pl.pallas_call(
        functools.partial(_in_proj_kernel, norm=h.dtype == F32, ncast=len(later_weights)),
        grid=(n // tn, nr),
        in_specs=[pl.BlockSpec((tm, k), lambda c, r: (r, 0)),
                  pl.BlockSpec((None, 1, k), lambda c, r: (layer, 0, 0)),
                  pl.BlockSpec((None, k, tn), lambda c, r: (j, 0, c))] + cast_specs_in,
        out_specs=tuple([pl.BlockSpec((tm, tn), lambda c, r: (r, c))] + cast_specs_out),
        out_shape=tuple([jax.ShapeDtypeStruct((t, n), BF16)] + cast_shapes),
        scratch_shapes=[pltpu.VMEM((k, tn), BF16)],
        compiler_params=_cparams(),
        name="in_proj",
    )(h, gains, w_all, *later_weights)
    return outs[0], outs[1:]


def _pool_kernel(u_ref, wrap_ref, z_ref, wg_ref, sc_ref, o_ref, wb_ref, hist_ref, p_ref,
                 *, rows, cg):
    g = pl.program_id(0)
    b = pl.program_id(1)
    i = pl.program_id(2)

    @pl.when((i == 0) & (b == 0))
    def _():
        wb_ref[...] = wg_ref[...].astype(BF16)

    for gi, w in enumerate(POOL_WINDOWS):
        @pl.when(g == gi)
        def _(w=w):
            _pool_step(u_ref, wrap_ref, z_ref, sc_ref, o_ref, wb_ref, hist_ref, p_ref, i,
                       w=w, rows=rows, cg=cg)


def _pool_step(u_ref, wrap_ref, z_ref, sc_ref, o_ref, wb_ref, hist_ref, p_ref, i, *, w, rows, cg):
    @pl.when(i == 0)
    def _():
        seg0 = lax.broadcasted_iota(jnp.int32, (rows, 1), 0) == 0
        for m in range(1, w):
            blk = wrap_ref[CHUNK - m].astype(F32)
            hist_ref[m - 1] = jnp.where(seg0, 0.0, pltpu.roll(blk, 1, axis=0))

    def strip(rs, cs, first_tile):
        cur = [u_ref[t, rs, cs].astype(F32) for t in range(CHUNK)]
        s = {t: cur[t] for t in range(CHUNK)}
        for m in range(1, w):
            s[-m] = hist_ref[m - 1, rs, cs]
        lo, k = -(w - 1), 1
        while k < w:
            s = {t: s[t] + s[t - k] for t in range(lo + k, CHUNK)}
            lo += k
            k *= 2
        if first_tile:
            seq_start = (lax.broadcasted_iota(jnp.int32, (PACK, 1), 0) == 0) & (i == 0)
        for t in range(CHUNK):
            inv = 1.0 / w
            if first_tile:
                inv = jnp.where(seq_start, 1.0 / min(t + 1, w), inv)
            p_ref[t, rs, cs] = (s[t] * inv - cur[t]).astype(BF16)

    for r in range(rows // PACK):
        rs = slice(r * PACK, (r + 1) * PACK)
        for c in range(cg // LANES):
            strip(rs, slice(c * LANES, (c + 1) * LANES), r == 0)
        p = p_ref[:, rs, :].reshape(CHUNK * PACK, cg)
        m = jnp.dot(p, wb_ref[...], preferred_element_type=F32)
        zz = z_ref[:, rs, :].reshape(CHUNK * PACK, cg).astype(F32)
        y = m * sc_ref[...] * _silu(zz)
        o_ref[:, rs, :] = y.astype(o_ref.dtype).reshape(CHUNK, PACK, cg)

    for m in range(1, w):
        hist_ref[m - 1] = u_ref[CHUNK - m].astype(F32)


def _pool_mixer(uz, w_grp_all, scale_all, j, nb, ni):
    t, e2 = uz.shape
    e = e2 // 2
    ng, cg = w_grp_all.shape[1], w_grp_all.shape[2]
    assert ng == len(POOL_WINDOWS)
    nc = t // CHUNK
    rows = nc // (ni * nb)
    uz3 = uz.reshape(CHUNK, nc, e2)
    blk = (CHUNK, rows, cg)
    out = pl.pallas_call(
        functools.partial(_pool_kernel, rows=rows, cg=cg),
        grid=(ng, nb, ni),
        in_specs=[pl.BlockSpec(blk, lambda g, b, i: (0, i * nb + b, g)),
                  pl.BlockSpec(blk, lambda g, b, i: (0, (ni - 1) * nb + b, g)),
                  pl.BlockSpec(blk, lambda g, b, i: (0, i * nb + b, ng + g)),
                  pl.BlockSpec((None, None, cg, cg), lambda g, b, i: (j, g, 0, 0)),
                  pl.BlockSpec((None, 1, cg), lambda g, b, i: (j, 0, g))],
        out_specs=pl.BlockSpec(blk, lambda g, b, i: (0, i * nb + b, g)),
        out_shape=jax.ShapeDtypeStruct((CHUNK, nc, e), BF16),
        scratch_shapes=[pltpu.VMEM((cg, cg), BF16),
                        pltpu.VMEM((max(POOL_WINDOWS) - 1, rows, cg), F32),
                        pltpu.VMEM((CHUNK, rows, cg), BF16)],
        compiler_params=_cparams(),
        name="pool_mixer",
    )(uz3, uz3, uz3, w_grp_all, scale_all)
    return out.reshape(t, e)


def _causal_mask():
    n = CHUNK * SSM_P
    r_tau = lax.broadcasted_iota(jnp.int32, (n, n), 0) // SSM_P
    c_tau = lax.broadcasted_iota(jnp.int32, (n, n), 1) // SSM_P
    return jnp.where(r_tau >= c_tau, 1.0, 0.0).astype(BF16)


def _gen_group_weights(tab, bp, bq, cr, ci, mask):
    def re(k):
        return jnp.broadcast_to(tab[CHUNK - k:CHUNK - k + 1, :], (SSM_P, 2 * SSM_N))

    def im(k):
        return jnp.broadcast_to(tab[3 * CHUNK - k:3 * CHUNK - k + 1, :], (SSM_P, 2 * SSM_N))

    low = lax.broadcasted_iota(jnp.int32, (SSM_P, 2 * SSM_N), 1) < SSM_N
    bqs = jnp.where(low, -bq, bq)
    ca = jnp.where(low, cr, -ci)
    cb_ = jnp.where(low, -ci, -cr)
    wb = jnp.concatenate([bp * re(CHUNK - 1 - t) + bqs * im(CHUNK - 1 - t) for t in range(CHUNK)], axis=0)
    rn = jnp.concatenate([bp * re(-t) + bqs * im(-t) for t in range(CHUNK)], axis=0)
    cb = [ca * re(d) + cb_ * im(d) for d in range(CHUNK + 1)]
    lm = jnp.concatenate(cb[:CHUNK], axis=0)
    wc = jnp.concatenate(cb[1:], axis=0)
    lm_h, rn_h = lm.astype(BF16), rn.astype(BF16)
    lm_l = (lm - lm_h.astype(F32)).astype(BF16)
    rn_l = (rn - rn_h.astype(F32)).astype(BF16)
    kt = lax.dot_general(jnp.concatenate([lm_h, lm_h, lm_l], axis=1),
                         jnp.concatenate([rn_h, rn_l, rn_h], axis=1),
                         (((1,), (1,)), ((), ())), preferred_element_type=F32)
    return wb.T.astype(BF16), wc.astype(BF16), kt.astype(BF16) * mask


def _ssm_kernel(u_ref, tab_ref, btr_ref, bti_ref, cre_ref, cim_ref, c1_ref, c2_ref, d_ref, o_ref,
                xt_ref, ot_ref, kw_ref, st_ref, a_ref, ep_ref, eq_ref, hp_ref, mask_ref,
                *, gb, nc, ni, segs):
    n = SSM_N
    kk = CHUNK * SSM_P
    nbat = LANES // segs

    @pl.when(pl.program_id(0) == 0)
    def _():
        mask_ref[...] = _causal_mask()

    for t in range(CHUNK):
        xt_ref[t] = u_ref[t].T

    def cmul(ar, ai, xr, xi):
        return ar * xr - ai * xi, ar * xi + ai * xr

    def group_rows(j):
        return pl.ds(pl.multiple_of(j * SSM_P, SSM_P), SSM_P)

    def scan_rows(j, b):
        return pl.ds(pl.multiple_of((b * gb + j) * segs, segs), segs)

    def lanes2(x, y):
        return jnp.concatenate([x, y], axis=1)

    def tiles(st):
        return ([st[:n, i * LANES:(i + 1) * LANES] for i in range(ni)],
                [st[n:, i * LANES:(i + 1) * LANES] for i in range(ni)])

    def head(j, carry):
        ut = xt_ref[:, group_rows(j), :].reshape(kk, nc)
        t64 = tab_ref[j]
        tab = lanes2(t64, t64)
        btr, bti, cre, cim = btr_ref[j], bti_ref[j], cre_ref[j], cim_ref[j]
        wbt, wct, ktt = _gen_group_weights(tab, lanes2(btr, bti), lanes2(bti, btr),
                                           lanes2(cre, cre), lanes2(cim, cim), mask_ref[...])
        kw_ref[j, :, :kk] = ktt
        kw_ref[j, :, kk:] = wct
        st = jnp.dot(wbt, ut, preferred_element_type=F32)
        st_ref[j] = st
        s_r, s_i = tiles(st)
        rows8 = jnp.concatenate([tab[0:8], tab[2 * CHUNK:2 * CHUNK + 8]], axis=0)
        cols = jnp.concatenate([rows8] * (LANES // 16), axis=0).T
        ar = jnp.broadcast_to(cols[:n, 0:1], (n, LANES))
        ai = jnp.broadcast_to(cols[:n, 8:9], (n, LANES))
        a_ref[j] = jnp.concatenate([ar, ai], axis=0)
        er, ei = s_r[0], s_i[0]
        for i in range(1, ni):
            mr, mi = cmul(ar, ai, er, ei)
            er, ei = mr + s_r[i], mi + s_i[i]
        ep = jnp.concatenate([er, ei], axis=0).T
        eq = jnp.concatenate([ei, er], axis=0).T
        for b in range(nbat):
            ep_ref[scan_rows(j, b), :] = ep[b * segs:(b + 1) * segs]
            eq_ref[scan_rows(j, b), :] = eq[b * segs:(b + 1) * segs]
        return carry

    lax.fori_loop(0, gb, head, 0, unroll=4)

    c1 = jnp.concatenate([c1_ref[...]] * nbat, axis=0)
    c2 = jnp.concatenate([c2_ref[...]] * nbat, axis=0)
    pairs = gb * nbat

    def seg_step(s, carry):
        p, q = carry
        at = pl.ds(s, pairs, stride=segs)
        hp_ref[at, :] = p
        return (c1 * p + c2 * q + ep_ref[at, :], c1 * q - c2 * p + eq_ref[at, :])

    zero = jnp.zeros((pairs, 2 * n), F32)
    lax.fori_loop(0, segs, seg_step, (zero, zero), unroll=8)

    def tail(j, carry):
        rows = group_rows(j)
        ut = xt_ref[:, rows, :].reshape(kk, nc)
        s_r, s_i = tiles(st_ref[j])
        a = a_ref[j]
        ar, ai = a[:n], a[n:]
        e = jnp.concatenate([hp_ref[scan_rows(j, b), :] for b in range(nbat)], axis=0).T
        hr, hi = e[:n], e[n:]
        h_tiles = []
        for i in range(ni):
            h_tiles.append(jnp.concatenate([hr, hi], axis=0).astype(BF16))
            if i + 1 < ni:
                mr, mi = cmul(ar, ai, hr, hi)
                hr, hi = mr + s_r[i], mi + s_i[i]
        ht = jnp.concatenate(h_tiles, axis=1)
        yt = jnp.dot(kw_ref[j], jnp.concatenate([ut, ht], axis=0), preferred_element_type=F32)
        skip = jnp.concatenate([d_ref[j]] * ni, axis=1)
        y3 = yt.reshape(CHUNK, SSM_P, nc) + skip[None] * ut.astype(F32).reshape(CHUNK, SSM_P, nc)
        ot_ref[:, rows, :] = _gelu_tanh(y3)
        return carry

    lax.fori_loop(0, gb, tail, 0, unroll=4)

    for t in range(CHUNK):
        o_ref[t] = ot_ref[t].T.astype(o_ref.dtype)


def _ssm_mixer(uz, tables, j, nb, ni, gb=8):
    tab, bt_r, bt_i, c_re, c_im, c1, c2, d_tab = tables
    small = pl.BlockSpec((None, gb, SSM_P, SSM_N), lambda i: (j, i, 0, 0))
    row = pl.BlockSpec((None, gb, 2 * SSM_N), lambda i: (j, i, 0))
    t, e2 = uz.shape
    e = e2 // 2
    g = e // SSM_P
    nc = t // CHUNK
    segs = LANES // nb
    uz3 = uz.reshape(CHUNK, nc, e2)
    wl = gb * SSM_P
    kern = functools.partial(_ssm_kernel, gb=gb, nc=nc, ni=ni, segs=segs)
    out = pl.pallas_call(
        kern,
        grid=(g // gb,),
        in_specs=[pl.BlockSpec((CHUNK, nc, wl), lambda i: (0, 0, i)),
                  pl.BlockSpec((None, gb, _T_POW_ROWS, SSM_N), lambda i: (j, i, 0, 0)),
                  small, small, small, small, row, row,
                  pl.BlockSpec((None, gb, SSM_P, LANES), lambda i: (j, i, 0, 0))],
        out_specs=pl.BlockSpec((CHUNK, nc, wl), lambda i: (0, 0, i)),
        out_shape=jax.ShapeDtypeStruct((CHUNK, nc, e), BF16),
        scratch_shapes=[pltpu.VMEM((CHUNK, wl, nc), BF16),
                        pltpu.VMEM((CHUNK, wl, nc), F32),
                        pltpu.VMEM((gb, CHUNK * SSM_P, CHUNK * SSM_P + 2 * SSM_N), BF16),
                        pltpu.VMEM((gb, 2 * SSM_N, nc), F32),
                        pltpu.VMEM((gb, 2 * SSM_N, LANES), F32),
                        pltpu.VMEM((gb * LANES, 2 * SSM_N), F32),
                        pltpu.VMEM((gb * LANES, 2 * SSM_N), F32),
                        pltpu.VMEM((gb * LANES, 2 * SSM_N), F32),
                        pltpu.VMEM((CHUNK * SSM_P, CHUNK * SSM_P), BF16)],
        compiler_params=_cparams(),
        name="ssm_mixer",
    )(uz3, tab, bt_r, bt_i, c_re, c_im, c1, c2, d_tab)
    return out.reshape(t, e)


def _ssm_tables(a_re, a_im, log_dt, b_re, b_im, c_re, c_im, d_skip, seg_tokens):
    dt = jnp.exp(log_dt)[..., None]
    la, th = a_re * dt, a_im * dt
    mag = jnp.exp(la)
    abr = mag * jnp.cos(th)
    abi = mag * jnp.sin(th)
    den = a_re * a_re + a_im * a_im
    nr = abr - 1.0
    fr = (nr * a_re + abi * a_im) / den
    fi = (abi * a_re - nr * a_im) / den
    bbr = fr[..., None] * b_re - fi[..., None] * b_im
    bbi = fr[..., None] * b_im + fi[..., None] * b_re

    def cat(x, y):
        return jnp.concatenate([x, y], axis=-1)

    def power(k, la_, th_):
        m = jnp.exp(k * la_)
        return m * jnp.cos(k * th_), m * jnp.sin(k * th_)

    k = (CHUNK - np.arange(2 * CHUNK)).astype(np.float32)
    pr, pi = power(k[:, None], la[:, :, None, :], th[:, :, None, :])
    tab = jnp.concatenate([pr, pi], axis=2)
    bt_r, bt_i = jnp.swapaxes(bbr, -1, -2), jnp.swapaxes(bbi, -1, -2)

    sr, si = power(float(seg_tokens), la, th)
    d_tab = jnp.broadcast_to(d_skip.reshape(d_skip.shape[0], -1, SSM_P, 1),
                             (d_skip.shape[0], d_skip.shape[1] // SSM_P, SSM_P, LANES))
    return tab, bt_r, bt_i, c_re, c_im, cat(sr, sr), cat(-si, si), d_tab


def _glu_kernel(a_ref, w_ref, b_ref, g_ref, z_ref, o_ref):
    nsub = o_ref.shape[1] // GLU_SUB

    def matmul(s):
        return jnp.dot(a_ref[...], w_ref[:, s * GLU_SUB:(s + 1) * GLU_SUB],
                       preferred_element_type=F32)

    acc = matmul(0)
    for s in range(nsub):
        cs = slice(s * GLU_SUB, (s + 1) * GLU_SUB)
        nxt = matmul(s + 1) if s + 1 < nsub else None
        gg = g_ref[:, cs].astype(F32)
        zz = z_ref[:, cs].astype(F32)
        den = (1.0 + jnp.exp2(-LOG2E * (acc + b_ref[:, cs]))) * (1.0 + jnp.exp2(-LOG2E * zz))
        o_ref[:, cs] = (gg * zz / den).astype(o_ref.dtype)
        acc = nxt


def _glu(gact, w, b_all, uz, j, tm=1024, tn=1024):
    t, e = gact.shape
    zoff = e // tn
    return pl.pallas_call(
        _glu_kernel,
        grid=(e // tn, t // tm),
        in_specs=[pl.BlockSpec((tm, e), lambda c, r: (r, 0)),
                  pl.BlockSpec((e, tn), lambda c, r: (0, c)),
                  pl.BlockSpec((None, 1, tn), lambda c, r: (j, 0, c)),
                  pl.BlockSpec((tm, tn), lambda c, r: (r, c)),
                  pl.BlockSpec((tm, tn), lambda c, r: (r, c + zoff))],
        out_specs=pl.BlockSpec((tm, tn), lambda c, r: (r, c)),
        out_shape=jax.ShapeDtypeStruct((t, e), BF16),
        compiler_params=_cparams(),
        name="glu",
    )(gact, w, b_all, gact, uz)


def _out_proj_kernel(y_ref, w_ref, x_ref, g_ref, *o_refs, last):
    xn = x_ref[...] + jnp.dot(y_ref[...], w_ref[...], preferred_element_type=F32)
    inv = lax.rsqrt(jnp.mean(xn * xn, axis=-1, keepdims=True) + NORM_EPS)
    hn = xn * inv * g_ref[...]
    if last:
        o_refs[0][...] = hn
    else:
        o_refs[0][...] = xn
        o_refs[1][...] = hn.astype(o_refs[1].dtype)


def _out_proj(y, w, x, gains, layer, last, tm=512):
    t, e = y.shape
    d = w.shape[1]
    row = pl.BlockSpec((tm, d), lambda i: (i, 0))
    if last:
        out_shape = jax.ShapeDtypeStruct((t, d), F32)
        out_specs = row
    else:
        out_shape = (jax.ShapeDtypeStruct((t, d), F32), jax.ShapeDtypeStruct((t, d), BF16))
        out_specs = (row, row)
    return pl.pallas_call(
        functools.partial(_out_proj_kernel, last=last),
        grid=(t // tm,),
        in_specs=[pl.BlockSpec((tm, e), lambda i: (i, 0)),
                  pl.BlockSpec((e, d), lambda i: (0, 0), pipeline_mode=pl.Buffered(1)),
                  row,
                  pl.BlockSpec((None, 1, d), lambda i: (layer, 0, 0))],
        out_specs=out_specs,
        out_shape=out_shape,
        compiler_params=_cparams(),
        name="out_proj",
    )(y, w, x, gains)


def kernel(x, norm_g, final_norm_g, pool_w_in, pool_w_grp, pool_scale, pool_w_out, ssm_w_in, ssm_a_re, ssm_a_im, ssm_log_dt, ssm_b_re, ssm_b_im, ssm_c_re, ssm_c_im, ssm_d, ssm_w_glu, ssm_b_glu, ssm_w_out):
    nb, seq_len, d = x.shape
    depth = norm_g.shape[0]
    segs = LANES // nb
    ni = seq_len // (CHUNK * segs)
    assert nb * segs == LANES and ni * segs * CHUNK == seq_len
    t = nb * seq_len

    gains = jnp.concatenate([norm_g, final_norm_g[None]], axis=0)[:, None, :]
    pool_scale3 = pool_scale[:, None, :]
    ssm_b_glu3 = ssm_b_glu[:, None, :]
    ssm_tables = _ssm_tables(ssm_a_re, ssm_a_im, ssm_log_dt, ssm_b_re, ssm_b_im,
                             ssm_c_re, ssm_c_im, ssm_d, ni * CHUNK)

    xs = x.reshape(nb, segs, ni, CHUNK, d).transpose(3, 2, 0, 1, 4).reshape(t, d)
    h = xs
    for layer in range(depth):
        j = layer // 2
        if layer % 2 == 0:
            uz, (w_out,) = _in_proj(h, gains, layer, pool_w_in, j, [pool_w_out])
            y = _pool_mixer(uz, pool_w_grp, pool_scale3, j, nb, ni)
        else:
            uz, (w_out, w_glu) = _in_proj(h, gains, layer, ssm_w_in, j, [ssm_w_out, ssm_w_glu])
            gact = _ssm_mixer(uz, ssm_tables, j, nb, ni)
            y = _glu(gact, w_glu, ssm_b_glu3, uz, j)
        last = layer == depth - 1
        res = _out_proj(y, w_out, xs, gains, layer + 1, last)
        if last:
            return res.reshape(CHUNK, ni, nb, segs, d).transpose(2, 3, 1, 0, 4).reshape(nb, seq_len, d)
        xs, h = res
```

```python
import functools
import math

import jax
import jax.numpy as jnp
import numpy as np
from jax import lax
from jax.experimental import pallas as pl
from jax.experimental.pallas import tpu as pltpu

F32 = jnp.float32
BF16 = jnp.bfloat16

POOL_WINDOWS = (2, 4, 8, 16)
SSM_P = 16
SSM_N = 64
CHUNK = 16
LANES = 128
GLU_SUB = 256
PACK = 16
LOG2E = math.log2(math.e)
NORM_EPS = 1e-6
VMEM_LIMIT = 56 * 1024 * 1024


def _cparams():
    return pltpu.CompilerParams(vmem_limit_bytes=VMEM_LIMIT)


def _silu(z):
    return z * jax.nn.sigmoid(z)


def _gelu_tanh(y):
    a = -2.0 * math.sqrt(2.0 / math.pi) * math.log2(math.e)
    return y / (1.0 + jnp.exp2(y * (a + (a * 0.044715) * (y * y))))


def _in_proj_kernel(*refs, norm, ncast):
    a_ref, g_ref, w_ref = refs[:3]
    cast_in, o_ref = refs[3:3 + ncast], refs[3 + ncast]
    cast_out, wb_ref = refs[4 + ncast:4 + 2 * ncast], refs[4 + 2 * ncast]

    @pl.when(pl.program_id(1) == 0)
    def _():
        wb_ref[...] = w_ref[...].astype(BF16)

    a = a_ref[...]
    if norm:
        inv = lax.rsqrt(jnp.mean(a * a, axis=-1, keepdims=True) + NORM_EPS)
        a = (a * inv * g_ref[...]).astype(BF16)
    o_ref[...] = jnp.dot(a, wb_ref[...], preferred_element_type=F32).astype(o_ref.dtype)
    for src, dst in zip(cast_in, cast_out):
        dst[...] = src[...].astype(BF16)


def _in_proj(h, gains, layer, w_all, j, later_weights, tm=1024, tn=1024):
    t, k = h.shape
    n = w_all.shape[2]
    nr = t // tm
    steps = (n // tn) * nr
    cast_specs_in, cast_specs_out, cast_shapes = [], [], []
    for w in later_weights:
        rows, cols = w.shape[1], w.shape[2]
        ws = rows // steps
        cast_specs_in.append(pl.BlockSpec((None, ws, cols), lambda c, r: (j, c * nr + r, 0)))
        cast_specs_out.append(pl.BlockSpec((ws, cols), lambda c, r: (c * nr + r, 0)))
        cast_shapes.append(jax.ShapeDtypeStruct((rows, cols), BF16))
    outs = pl.pallas_call(
        functools.partial(_in_proj_kernel, norm=h.dtype == F32, ncast=len(later_weights)),
        grid=(n // tn, nr),
        in_specs=[pl.BlockSpec((tm, k), lambda c, r: (r, 0)),
                  pl.BlockSpec((None, 1, k), lambda c, r: (layer, 0, 0)),
                  pl.BlockSpec((None, k, tn), lambda c, r: (j, 0, c))] + cast_specs_in,
        out_specs=tuple([pl.BlockSpec((tm, tn), lambda c, r: (r, c))] + cast_specs_out),
        out_shape=tuple([jax.ShapeDtypeStruct((t, n), BF16)] + cast_shapes),
        scratch_shapes=[pltpu.VMEM((k, tn), BF16)],
        compiler_params=_cparams(),
        name="in_proj",
    )(h, gains, w_all, *later_weights)
    return outs[0], outs[1:]


def _pool_kernel(u_ref, wrap_ref, z_ref, wg_ref, sc_ref, o_ref, wb_ref, hist_ref, p_ref,
                 *, rows, cg):
    g = pl.program_id(0)
    b = pl.program_id(1)
    i = pl.program_id(2)

    @pl.when((i == 0) & (b == 0))
    def _():
        wb_ref[...] = wg_ref[...].astype(BF16)

    for gi, w in enumerate(POOL_WINDOWS):
        @pl.when(g == gi)
        def _(w=w):
            _pool_step(u_ref, wrap_ref, z_ref, sc_ref, o_ref, wb_ref, hist_ref, p_ref, i,
                       w=w, rows=rows, cg=cg)


def _pool_step(u_ref, wrap_ref, z_ref, sc_ref, o_ref, wb_ref, hist_ref, p_ref, i, *, w, rows, cg):
    @pl.when(i == 0)
    def _():
        seg0 = lax.broadcasted_iota(jnp.int32, (rows, 1), 0) == 0
        for m in range(1, w):
            blk = wrap_ref[CHUNK - m].astype(F32)
            hist_ref[m - 1] = jnp.where(seg0, 0.0, pltpu.roll(blk, 1, axis=0))

    def strip(rs, cs, first_tile):
        cur = [u_ref[t, rs, cs].astype(F32) for t in range(CHUNK)]
        s = {t: cur[t] for t in range(CHUNK)}
        for m in range(1, w):
            s[-m] = hist_ref[m - 1, rs, cs]
        lo, k = -(w - 1), 1
        while k < w:
            s = {t: s[t] + s[t - k] for t in range(lo + k, CHUNK)}
            lo += k
            k *= 2
        if first_tile:
            seq_start = (lax.broadcasted_iota(jnp.int32, (PACK, 1), 0) == 0) & (i == 0)
        for t in range(CHUNK):
            inv = 1.0 / w
            if first_tile:
                inv = jnp.where(seq_start, 1.0 / min(t + 1, w), inv)
            p_ref[t, rs, cs] = (s[t] * inv - cur[t]).astype(BF16)

    for r in range(rows // PACK):
        rs = slice(r * PACK, (r + 1) * PACK)
        for c in range(cg // LANES):
            strip(rs, slice(c * LANES, (c + 1) * LANES), r == 0)
        p = p_ref[:, rs, :].reshape(CHUNK * PACK, cg)
        m = jnp.dot(p, wb_ref[...], preferred_element_type=F32)
        zz = z_ref[:, rs, :].reshape(CHUNK * PACK, cg).astype(F32)
        y = m * sc_ref[...] * _silu(zz)
        o_ref[:, rs, :] = y.astype(o_ref.dtype).reshape(CHUNK, PACK, cg)

    for m in range(1, w):
        hist_ref[m - 1] = u_ref[CHUNK - m].astype(F32)


def _pool_mixer(uz, w_grp_all, scale_all, j, nb, ni):
    t, e2 = uz.shape
    e = e2 // 2
    ng, cg = w_grp_all.shape[1], w_grp_all.shape[2]
    assert ng == len(POOL_WINDOWS)
    nc = t // CHUNK
    rows = nc // (ni * nb)
    uz3 = uz.reshape(CHUNK, nc, e2)
    blk = (CHUNK, rows, cg)
    out = pl.pallas_call(
        functools.partial(_pool_kernel, rows=rows, cg=cg),
        grid=(ng, nb, ni),
        in_specs=[pl.BlockSpec(blk, lambda g, b, i: (0, i * nb + b, g)),
                  pl.BlockSpec(blk, lambda g, b, i: (0, (ni - 1) * nb + b, g)),
                  pl.BlockSpec(blk, lambda g, b, i: (0, i * nb + b, ng + g)),
                  pl.BlockSpec((None, None, cg, cg), lambda g, b, i: (j, g, 0, 0)),
                  pl.BlockSpec((None, 1, cg), lambda g, b, i: (j, 0, g))],
        out_specs=pl.BlockSpec(blk, lambda g, b, i: (0, i * nb + b, g)),
        out_shape=jax.ShapeDtypeStruct((CHUNK, nc, e), BF16),
        scratch_shapes=[pltpu.VMEM((cg, cg), BF16),
                        pltpu.VMEM((max(POOL_WINDOWS) - 1, rows, cg), F32),
                        pltpu.VMEM((CHUNK, rows, cg), BF16)],
        compiler_params=_cparams(),
        name="pool_mixer",
    )(uz3, uz3, uz3, w_grp_all, scale_all)
    return out.reshape(t, e)


def _causal_mask():
    n = CHUNK * SSM_P
    r_tau = lax.broadcasted_iota(jnp.int32, (n, n), 0) // SSM_P
    c_tau = lax.broadcasted_iota(jnp.int32, (n, n), 1) // SSM_P
    return jnp.where(r_tau >= c_tau, 1.0, 0.0).astype(BF16)


def _gen_group_weights(tab, bp, bq, cr, ci, mask):
    def re(k):
        return jnp.broadcast_to(tab[CHUNK - k:CHUNK - k + 1, :], (SSM_P, 2 * SSM_N))

    def im(k):
        return jnp.broadcast_to(tab[3 * CHUNK - k:3 * CHUNK - k + 1, :], (SSM_P, 2 * SSM_N))

    low = lax.broadcasted_iota(jnp.int32, (SSM_P, 2 * SSM_N), 1) < SSM_N
    bqs = jnp.where(low, -bq, bq)
    ca = jnp.where(low, cr, -ci)
    cb_ = jnp.where(low, -ci, -cr)
    wb = jnp.concatenate([bp * re(CHUNK - 1 - t) + bqs * im(CHUNK - 1 - t) for t in range(CHUNK)], axis=0)
    rn = jnp.concatenate([bp * re(-t) + bqs * im(-t) for t in range(CHUNK)], axis=0)
    cb = [ca * re(d) + cb_ * im(d) for d in range(CHUNK + 1)]
    lm = jnp.concatenate(cb[:CHUNK], axis=0)
    wc = jnp.concatenate(cb[1:], axis=0)
    lm_h, rn_h = lm.astype(BF16), rn.astype(BF16)
    lm_l = (lm - lm_h.astype(F32)).astype(BF16)
    rn_l = (rn - rn_h.astype(F32)).astype(BF16)
    kt = lax.dot_general(jnp.concatenate([lm_h, lm_h, lm_l], axis=1),
                         jnp.concatenate([rn_h, rn_l, rn_h], axis=1),
                         (((1,), (1,)), ((), ())), preferred_element_type=F32)
    return wb.T.astype(BF16), wc.astype(BF16), kt.astype(BF16) * mask


def _ssm_kernel(u_ref, tab_ref, btr_ref, bti_ref, cre_ref, cim_ref, c1_ref, c2_ref, d_ref, o_ref,
                xt_ref, ot_ref, kw_ref, st_ref, a_ref, ep_ref, eq_ref, hp_ref, mask_ref,
                *, gb, nc, ni, segs):
    n = SSM_N
    kk = CHUNK * SSM_P
    nbat = LANES // segs

    @pl.when(pl.program_id(0) == 0)
    def _():
        mask_ref[...] = _causal_mask()

    for t in range(CHUNK):
        xt_ref[t] = u_ref[t].T

    def cmul(ar, ai, xr, xi):
        return ar * xr - ai * xi, ar * xi + ai * xr

    def group_rows(j):
        return pl.ds(pl.multiple_of(j * SSM_P, SSM_P), SSM_P)

    def scan_rows(j, b):
        return pl.ds(pl.multiple_of((b * gb + j) * segs, segs), segs)

    def lanes2(x, y):
        return jnp.concatenate([x, y], axis=1)

    def tiles(st):
        return ([st[:n, i * LANES:(i + 1) * LANES] for i in range(ni)],
                [st[n:, i * LANES:(i + 1) * LANES] for i in range(ni)])

    def head(j, carry):
        ut = xt_ref[:, group_rows(j), :].reshape(kk, nc)
        t = tab_ref[j]
        tsw = pltpu.roll(t, n, axis=1)
        low = lax.broadcasted_iota(jnp.int32, t.shape, 1) < n
        tab = jnp.concatenate([jnp.where(low, t, tsw), jnp.where(low, tsw, t)], axis=0)
        btr, bti, cre, cim = btr_ref[j], bti_ref[j], cre_ref[j], cim_ref[j]
        wbt, wct, ktt = _gen_group_weights(tab, lanes2(btr, bti), lanes2(bti, btr),
                                           lanes2(cre, cre), lanes2(cim, cim), mask_ref[...])
        kw_ref[j, :, :kk] = ktt
        kw_ref[j, :, kk:] = wct
        st = jnp.dot(wbt, ut, preferred_element_type=F32)
        st_ref[j] = st
        s_r, s_i = tiles(st)
        rows8 = jnp.concatenate([tab[0:8], tab[2 * CHUNK:2 * CHUNK + 8]], axis=0)
        cols = jnp.concatenate([rows8] * (LANES // 16), axis=0).T
        ar = jnp.broadcast_to(cols[:n, 0:1], (n, LANES))
        ai = jnp.broadcast_to(cols[:n, 8:9], (n, LANES))
        a_ref[j] = jnp.concatenate([ar, ai], axis=0)
        er, ei = s_r[0], s_i[0]
        for i in range(1, ni):
            mr, mi = cmul(ar, ai, er, ei)
            er, ei = mr + s_r[i], mi + s_i[i]
        ep = jnp.concatenate([er, ei], axis=0).T
        eq = jnp.concatenate([ei, er], axis=0).T
        for b in range(nbat):
            ep_ref[scan_rows(j, b), :] = ep[b * segs:(b + 1) * segs]
            eq_ref[scan_rows(j, b), :] = eq[b * segs:(b + 1) * segs]
        return carry

    lax.fori_loop(0, gb, head, 0, unroll=4)

    c1 = jnp.concatenate([c1_ref[...]] * nbat, axis=0)
    c2 = jnp.concatenate([c2_ref[...]] * nbat, axis=0)
    pairs = gb * nbat

    def seg_step(s, carry):
        p, q = carry
        at = pl.ds(s, pairs, stride=segs)
        hp_ref[at, :] = p
        return (c1 * p + c2 * q + ep_ref[at, :], c1 * q - c2 * p + eq_ref[at, :])

    zero = jnp.zeros((pairs, 2 * n), F32)
    lax.fori_loop(0, segs, seg_step, (zero, zero), unroll=8)

    def tail(j, carry):
        rows = group_rows(j)
        ut = xt_ref[:, rows, :].reshape(kk, nc)
        s_r, s_i = tiles(st_ref[j])
        a = a_ref[j]
        ar, ai = a[:n], a[n:]
        e = jnp.concatenate([hp_ref[scan_rows(j, b), :] for b in range(nbat)], axis=0).T
        hr, hi = e[:n], e[n:]
        h_tiles = []
        for i in range(ni):
            h_tiles.append(jnp.concatenate([hr, hi], axis=0).astype(BF16))
            if i + 1 < ni:
                mr, mi = cmul(ar, ai, hr, hi)
                hr, hi = mr + s_r[i], mi + s_i[i]
        ht = jnp.concatenate(h_tiles, axis=1)
        yt = jnp.dot(kw_ref[j], jnp.concatenate([ut, ht], axis=0), preferred_element_type=F32)
        skip = jnp.concatenate([d_ref[j]] * ni, axis=1)
        y3 = yt.reshape(CHUNK, SSM_P, nc) + skip[None] * ut.astype(F32).reshape(CHUNK, SSM_P, nc)
        ot_ref[:, rows, :] = _gelu_tanh(y3)
        return carry

    lax.fori_loop(0, gb, tail, 0, unroll=4)

    for t in range(CHUNK):
        o_ref[t] = ot_ref[t].T.astype(o_ref.dtype)


def _ssm_mixer(uz, tables, j, nb, ni, gb=8):
    tab, bt_r, bt_i, c_re, c_im, c1, c2, d_tab = tables
    small = pl.BlockSpec((None, gb, SSM_P, SSM_N), lambda i: (j, i, 0, 0))
    row = pl.BlockSpec((None, gb, 2 * SSM_N), lambda i: (j, i, 0))
    t, e2 = uz.shape
    e = e2 // 2
    g = e // SSM_P
    nc = t // CHUNK
    segs = LANES // nb
    uz3 = uz.reshape(CHUNK, nc, e2)
    wl = gb * SSM_P
    kern = functools.partial(_ssm_kernel, gb=gb, nc=nc, ni=ni, segs=segs)
    out = pl.pallas_call(
        kern,
        grid=(g // gb,),
        in_specs=[pl.BlockSpec((CHUNK, nc, wl), lambda i: (0, 0, i)),
                  pl.BlockSpec((None, gb, 2 * CHUNK, 2 * SSM_N), lambda i: (j, i, 0, 0)),
                  small, small, small, small, row, row,
                  pl.BlockSpec((None, gb, SSM_P, LANES), lambda i: (j, i, 0, 0))],
        out_specs=pl.BlockSpec((CHUNK, nc, wl), lambda i: (0, 0, i)),
        out_shape=jax.ShapeDtypeStruct((CHUNK, nc, e), BF16),
        scratch_shapes=[pltpu.VMEM((CHUNK, wl, nc), BF16),
                        pltpu.VMEM((CHUNK, wl, nc), F32),
                        pltpu.VMEM((gb, CHUNK * SSM_P, CHUNK * SSM_P + 2 * SSM_N), BF16),
                        pltpu.VMEM((gb, 2 * SSM_N, nc), F32),
                        pltpu.VMEM((gb, 2 * SSM_N, LANES), F32),
                        pltpu.VMEM((gb * LANES, 2 * SSM_N), F32),
                        pltpu.VMEM((gb * LANES, 2 * SSM_N), F32),
                        pltpu.VMEM((gb * LANES, 2 * SSM_N), F32),
                        pltpu.VMEM((CHUNK * SSM_P, CHUNK * SSM_P), BF16)],
        compiler_params=_cparams(),
        name="ssm_mixer",
    )(uz3, tab, bt_r, bt_i, c_re, c_im, c1, c2, d_tab)
    return out.reshape(t, e)


def _ssm_tables(a_re, a_im, log_dt, b_re, b_im, c_re, c_im, d_skip, seg_tokens):
    dt = jnp.exp(log_dt)[..., None]
    la, th = a_re * dt, a_im * dt
    mag = jnp.exp(la)
    abr = mag * jnp.cos(th)
    abi = mag * jnp.sin(th)
    den = a_re * a_re + a_im * a_im
    nr = abr - 1.0
    fr = (nr * a_re + abi * a_im) / den
    fi = (abi * a_re - nr * a_im) / den
    bbr = fr[..., None] * b_re - fi[..., None] * b_im
    bbi = fr[..., None] * b_im + fi[..., None] * b_re

    def cat(x, y):
        return jnp.concatenate([x, y], axis=-1)

    def power(k, la_, th_):
        m = jnp.exp(k * la_)
        return m * jnp.cos(k * th_), m * jnp.sin(k * th_)

    k = (CHUNK - np.arange(2 * CHUNK)).astype(np.float32)[:, None]
    quarter = np.concatenate([np.full(a_re.shape[-1], np.pi / 2), np.zeros(a_re.shape[-1])])
    tab = (jnp.exp(k * cat(la, la)[:, :, None, :])
           * jnp.sin(k * cat(th, th)[:, :, None, :] + jnp.asarray(quarter, dtype=th.dtype)))
    bt_r, bt_i = jnp.swapaxes(bbr, -1, -2), jnp.swapaxes(bbi, -1, -2)

    sr, si = power(float(seg_tokens), la, th)
    d_tab = jnp.broadcast_to(d_skip.reshape(d_skip.shape[0], -1, SSM_P, 1),
                             (d_skip.shape[0], d_skip.shape[1] // SSM_P, SSM_P, LANES))
    return tab, bt_r, bt_i, c_re, c_im, cat(sr, sr), cat(-si, si), d_tab


def _glu_kernel(a_ref, w_ref, b_ref, g_ref, z_ref, o_ref):
    nsub = o_ref.shape[1] // GLU_SUB

    def matmul(s):
        return jnp.dot(a_ref[...], w_ref[:, s * GLU_SUB:(s + 1) * GLU_SUB],
                       preferred_element_type=F32)

    acc = matmul(0)
    for s in range(nsub):
        cs = slice(s * GLU_SUB, (s + 1) * GLU_SUB)
        nxt = matmul(s + 1) if s + 1 < nsub else None
        gg = g_ref[:, cs].astype(F32)
        zz = z_ref[:, cs].astype(F32)
        den = (1.0 + jnp.exp2(-LOG2E * (acc + b_ref[:, cs]))) * (1.0 + jnp.exp2(-LOG2E * zz))
        o_ref[:, cs] = (gg * zz / den).astype(o_ref.dtype)
        acc = nxt


def _glu(gact, w, b_all, uz, j, tm=1024, tn=1024):
    t, e = gact.shape
    zoff = e // tn
    return pl.pallas_call(
        _glu_kernel,
        grid=(e // tn, t // tm),
        in_specs=[pl.BlockSpec((tm, e), lambda c, r: (r, 0)),
                  pl.BlockSpec((e, tn), lambda c, r: (0, c)),
                  pl.BlockSpec((None, 1, tn), lambda c, r: (j, 0, c)),
                  pl.BlockSpec((tm, tn), lambda c, r: (r, c)),
                  pl.BlockSpec((tm, tn), lambda c, r: (r, c + zoff))],
        out_specs=pl.BlockSpec((tm, tn), lambda c, r: (r, c)),
        out_shape=jax.ShapeDtypeStruct((t, e), BF16),
        compiler_params=_cparams(),
        name="glu",
    )(gact, w, b_all, gact, uz)


def _out_proj_kernel(y_ref, w_ref, x_ref, g_ref, *o_refs, last):
    xn = x_ref[...] + jnp.dot(y_ref[...], w_ref[...], preferred_element_type=F32)
    inv = lax.rsqrt(jnp.mean(xn * xn, axis=-1, keepdims=True) + NORM_EPS)
    hn = xn * inv * g_ref[...]
    if last:
        o_refs[0][...] = hn
    else:
        o_refs[0][...] = xn
        o_refs[1][...] = hn.astype(o_refs[1].dtype)


def _out_proj(y, w, x, gains, layer, last, tm=512):
    t, e = y.shape
    d = w.shape[1]
    row = pl.BlockSpec((tm, d), lambda i: (i, 0))
    if last:
        out_shape = jax.ShapeDtypeStruct((t, d), F32)
        out_specs = row
    else:
        out_shape = (jax.ShapeDtypeStruct((t, d), F32), jax.ShapeDtypeStruct((t, d), BF16))
        out_specs = (row, row)
    return pl.pallas_call(
        functools.partial(_out_proj_kernel, last=last),
        grid=(t // tm,),
        in_specs=[pl.BlockSpec((tm, e), lambda i: (i, 0)),
                  pl.BlockSpec((e, d), lambda i: (0, 0), pipeline_mode=pl.Buffered(1)),
                  row,
                  pl.BlockSpec((None, 1, d), lambda i: (layer, 0, 0))],
        out_specs=out_specs,
        out_shape=out_shape,
        compiler_params=_cparams(),
        name="out_proj",
    )(y, w, x, gains)


def kernel(x, norm_g, final_norm_g, pool_w_in, pool_w_grp, pool_scale, pool_w_out, ssm_w_in, ssm_a_re, ssm_a_im, ssm_log_dt, ssm_b_re, ssm_b_im, ssm_c_re, ssm_c_im, ssm_d, ssm_w_glu, ssm_b_glu, ssm_w_out):
    nb, seq_len, d = x.shape
    depth = norm_g.shape[0]
    segs = LANES // nb
    ni = seq_len // (CHUNK * segs)
    assert nb * segs == LANES and ni * segs * CHUNK == seq_len
    t = nb * seq_len

    gains = jnp.concatenate([norm_g, final_norm_g[None]], axis=0)[:, None, :]
    pool_scale3 = pool_scale[:, None, :]
    ssm_b_glu3 = ssm_b_glu[:, None, :]
    ssm_tables = _ssm_tables(ssm_a_re, ssm_a_im, ssm_log_dt, ssm_b_re, ssm_b_im,
                             ssm_c_re, ssm_c_im, ssm_d, ni * CHUNK)

    xs = x.reshape(nb, segs, ni, CHUNK, d).transpose(3, 2, 0, 1, 4).reshape(t, d)
    h = xs
    for layer in range(depth):
        j = layer // 2
        if layer % 2 == 0:
            uz, (w_out,) = _in_proj(h, gains, layer, pool_w_in, j, [pool_w_out])
            y = _pool_mixer(uz, pool_w_grp, pool_scale3, j, nb, ni)
        else:
            uz, (w_out, w_glu) = _in_proj(h, gains, layer, ssm_w_in, j, [ssm_w_out, ssm_w_glu])
            gact = _ssm_mixer(uz, ssm_tables, j, nb, ni)
            y = _glu(gact, w_glu, ssm_b_glu3, uz, j)
        last = layer == depth - 1
        res = _out_proj(y, w_out, xs, gains, layer + 1, last)
        if last:
            return res.reshape(CHUNK, ni, nb, segs, d).transpose(2, 3, 1, 0, 4).reshape(nb, seq_len, d)
        xs, h = res
```

```python
import functools
import math

import jax
import jax.numpy as jnp
import numpy as np
from jax import lax
from jax.experimental import pallas as pl
from jax.experimental.pallas import tpu as pltpu

F32 = jnp.float32
BF16 = jnp.bfloat16

POOL_WINDOWS = (2, 4, 8, 16)
SSM_P = 16
SSM_N = 64
CHUNK = 16
LANES = 128
GLU_SUB = 256
PACK = 16
LOG2E = math.log2(math.e)
NORM_EPS = 1e-6
VMEM_LIMIT = 56 * 1024 * 1024


def _cparams():
    return pltpu.CompilerParams(vmem_limit_bytes=VMEM_LIMIT)


def _gelu_tanh(y):
    a = -2.0 * math.sqrt(2.0 / math.pi) * math.log2(math.e)
    return y / (1.0 + jnp.exp2(y * (a + (a * 0.044715) * (y * y))))


def _in_proj_kernel(*refs, norm, ncast):
    a_ref, g_ref, w_ref = refs[:3]
    cast_in, o_ref = refs[3:3 + ncast], refs[3 + ncast]
    cast_out, wb_ref = refs[4 + ncast:4 + 2 * ncast], refs[4 + 2 * ncast]

    @pl.when(pl.program_id(1) == 0)
    def _():
        wb_ref[...] = w_ref[...].astype(BF16)

    a = a_ref[...]
    if norm:
        inv = lax.rsqrt(jnp.mean(a * a, axis=-1, keepdims=True) + NORM_EPS)
        a = (a * inv * g_ref[...]).astype(BF16)
    o_ref[...] = jnp.dot(a, wb_ref[...], preferred_element_type=F32).astype(o_ref.dtype)
    for src, dst in zip(cast_in, cast_out):
        dst[...] = src[...].astype(BF16)


def _in_proj(h, gains, layer, w_all, j, later_weights, tm=1024, tn=1024):
    t, k = h.shape
    n = w_all.shape[2]
    nr = t // tm
    steps = (n // tn) * nr
    cast_specs_in, cast_specs_out, cast_shapes = [], [], []
    for w in later_weights:
        rows, cols = w.shape[1], w.shape[2]
        ws = rows // steps
        cast_specs_in.append(pl.BlockSpec((None, ws, cols), lambda c, r: (j, c * nr + r, 0)))
        cast_specs_out.append(pl.BlockSpec((ws, cols), lambda c, r: (c * nr + r, 0)))
        cast_shapes.append(jax.ShapeDtypeStruct((rows, cols), BF16))
    outs = pl.pallas_call(
        functools.partial(_in_proj_kernel, norm=h.dtype == F32, ncast=len(later_weights)),
        grid=(n // tn, nr),
        in_specs=[pl.BlockSpec((tm, k), lambda c, r: (r, 0)),
                  pl.BlockSpec((None, 1, k), lambda c, r: (layer, 0, 0)),
                  pl.BlockSpec((None, k, tn), lambda c, r: (j, 0, c))] + cast_specs_in,
        out_specs=tuple([pl.BlockSpec((tm, tn), lambda c, r: (r, c))] + cast_specs_out),
        out_shape=tuple([jax.ShapeDtypeStruct((t, n), BF16)] + cast_shapes),
        scratch_shapes=[pltpu.VMEM((k, tn), BF16)],
        compiler_params=_cparams(),
        name="in_proj",
    )(h, gains, w_all, *later_weights)
    return outs[0], outs[1:]


def _pool_kernel(u_ref, wrap_ref, z_ref, wg_ref, sc_ref, o_ref, wb_ref, hist_ref, p_ref,
                 *, rows, cg):
    g = pl.program_id(0)
    b = pl.program_id(1)
    i = pl.program_id(2)

    @pl.when((i == 0) & (b == 0))
    def _():
        wb_ref[...] = wg_ref[...].astype(BF16)

    for gi, w in enumerate(POOL_WINDOWS):
        @pl.when(g == gi)
        def _(w=w):
            _pool_step(u_ref, wrap_ref, z_ref, sc_ref, o_ref, wb_ref, hist_ref, p_ref, i,
                       w=w, rows=rows, cg=cg)


def _pool_step(u_ref, wrap_ref, z_ref, sc_ref, o_ref, wb_ref, hist_ref, p_ref, i, *, w, rows, cg):
    @pl.when(i == 0)
    def _():
        seg0 = lax.broadcasted_iota(jnp.int32, (rows, 1), 0) == 0
        for m in range(1, w):
            blk = wrap_ref[CHUNK - m].astype(F32)
            hist_ref[m - 1] = jnp.where(seg0, 0.0, pltpu.roll(blk, 1, axis=0)).astype(BF16)

    def strip(rs, cs, first_tile):
        cur = [u_ref[t, rs, cs].astype(F32) for t in range(CHUNK)]
        s = {t: cur[t] for t in range(CHUNK)}
        for m in range(1, w):
            s[-m] = hist_ref[m - 1, rs, cs].astype(F32)
        lo, k = -(w - 1), 1
        while k < w:
            s = {t: s[t] + s[t - k] for t in range(lo + k, CHUNK)}
            lo += k
            k *= 2
        if first_tile:
            seq_start = (lax.broadcasted_iota(jnp.int32, (PACK, 1), 0) == 0) & (i == 0)
        for t in range(CHUNK):
            inv = 1.0 / w
            if first_tile:
                inv = jnp.where(seq_start, 1.0 / min(t + 1, w), inv)
            p_ref[t, rs, cs] = (s[t] * inv - cur[t]).astype(BF16)

    for r in range(rows // PACK):
        rs = slice(r * PACK, (r + 1) * PACK)
        for c in range(cg // LANES):
            strip(rs, slice(c * LANES, (c + 1) * LANES), r == 0)
        p = p_ref[:, rs, :].reshape(CHUNK * PACK, cg)
        m = jnp.dot(p, wb_ref[...], preferred_element_type=F32)
        zz = z_ref[:, rs, :].reshape(CHUNK * PACK, cg).astype(F32)
        y = m * sc_ref[...] * zz / (1.0 + jnp.exp2(-LOG2E * zz))
        o_ref[:, rs, :] = y.astype(o_ref.dtype).reshape(CHUNK, PACK, cg)

    for m in range(1, w):
        hist_ref[m - 1] = u_ref[CHUNK - m]


def _pool_mixer(uz, w_grp_all, scale_all, j, nb, ni):
    t, e2 = uz.shape
    e = e2 // 2
    ng, cg = w_grp_all.shape[1], w_grp_all.shape[2]
    assert ng == len(POOL_WINDOWS)
    nc = t // CHUNK
    rows = nc // (ni * nb)
    uz3 = uz.reshape(CHUNK, nc, e2)
    blk = (CHUNK, rows, cg)
    out = pl.pallas_call(
        functools.partial(_pool_kernel, rows=rows, cg=cg),
        grid=(ng, nb, ni),
        in_specs=[pl.BlockSpec(blk, lambda g, b, i: (0, i * nb + b, g)),
                  pl.BlockSpec(blk, lambda g, b, i: (0, (ni - 1) * nb + b, g)),
                  pl.BlockSpec(blk, lambda g, b, i: (0, i * nb + b, ng + g)),
                  pl.BlockSpec((None, None, cg, cg), lambda g, b, i: (j, g, 0, 0)),
                  pl.BlockSpec((None, 1, cg), lambda g, b, i: (j, 0, g))],
        out_specs=pl.BlockSpec(blk, lambda g, b, i: (0, i * nb + b, g)),
        out_shape=jax.ShapeDtypeStruct((CHUNK, nc, e), BF16),
        scratch_shapes=[pltpu.VMEM((cg, cg), BF16),
                        pltpu.VMEM((max(POOL_WINDOWS) - 1, rows, cg), BF16),
                        pltpu.VMEM((CHUNK, rows, cg), BF16)],
        compiler_params=_cparams(),
        name="pool_mixer",
    )(uz3, uz3, uz3, w_grp_all, scale_all)
    return out.reshape(t, e)


def _causal_mask():
    n = CHUNK * SSM_P
    r_tau = lax.broadcasted_iota(jnp.int32, (n, n), 0) // SSM_P
    c_tau = lax.broadcasted_iota(jnp.int32, (n, n), 1) // SSM_P
    return jnp.where(r_tau >= c_tau, 1.0, 0.0).astype(BF16)


def _gen_group_weights(tab, bp, bq, cr, ci, mask):
    def re(k):
        return jnp.broadcast_to(tab[CHUNK - k:CHUNK - k + 1, :], (SSM_P, 2 * SSM_N))

    def im(k):
        return jnp.broadcast_to(tab[3 * CHUNK - k:3 * CHUNK - k + 1, :], (SSM_P, 2 * SSM_N))

    low = lax.broadcasted_iota(jnp.int32, (SSM_P, 2 * SSM_N), 1) < SSM_N
    bqs = jnp.where(low, -bq, bq)
    ca = jnp.where(low, cr, -ci)
    cb_ = jnp.where(low, -ci, -cr)
    wb = jnp.concatenate([bp * re(CHUNK - 1 - t) + bqs * im(CHUNK - 1 - t) for t in range(CHUNK)], axis=0)
    rn = jnp.concatenate([bp * re(-t) + bqs * im(-t) for t in range(CHUNK)], axis=0)
    cb = [ca * re(d) + cb_ * im(d) for d in range(CHUNK + 1)]
    lm = jnp.concatenate(cb[:CHUNK], axis=0)
    wc = jnp.concatenate(cb[1:], axis=0)
    lm_h, rn_h = lm.astype(BF16), rn.astype(BF16)
    lm_l = (lm - lm_h.astype(F32)).astype(BF16)
    rn_l = (rn - rn_h.astype(F32)).astype(BF16)
    kt = lax.dot_general(jnp.concatenate([lm_h, lm_h, lm_l], axis=1),
                         jnp.concatenate([rn_h, rn_l, rn_h], axis=1),
                         (((1,), (1,)), ((), ())), preferred_element_type=F32)
    return wb.T.astype(BF16), wc.astype(BF16), kt.astype(BF16) * mask


def _ssm_kernel(u_ref, tab_ref, btr_ref, bti_ref, cre_ref, cim_ref, c1_ref, c2_ref, d_ref, o_ref,
                xt_ref, ot_ref, kw_ref, st_ref, a_ref, ep_ref, eq_ref, hp_ref, mask_ref,
                *, gb, nc, ni, segs):
    n = SSM_N
    kk = CHUNK * SSM_P
    nbat = LANES // segs

    @pl.when(pl.program_id(0) == 0)
    def _():
        mask_ref[...] = _causal_mask()

    for t in range(CHUNK):
        xt_ref[t] = u_ref[t].T

    def cmul(ar, ai, xr, xi):
        return ar * xr - ai * xi, ar * xi + ai * xr

    def group_rows(j):
        return pl.ds(pl.multiple_of(j * SSM_P, SSM_P), SSM_P)

    def scan_rows(j, b):
        return pl.ds(pl.multiple_of((b * gb + j) * segs, segs), segs)

    def lanes2(x, y):
        return jnp.concatenate([x, y], axis=1)

    def tiles(st):
        return ([st[:n, i * LANES:(i + 1) * LANES] for i in range(ni)],
                [st[n:, i * LANES:(i + 1) * LANES] for i in range(ni)])

    def head(j, carry):
        ut = xt_ref[:, group_rows(j), :].reshape(kk, nc)
        t = tab_ref[j]
        tsw = pltpu.roll(t, n, axis=1)
        low = lax.broadcasted_iota(jnp.int32, t.shape, 1) < n
        tab = jnp.concatenate([jnp.where(low, t, tsw), jnp.where(low, tsw, t)], axis=0)
        btr, bti, cre, cim = btr_ref[j], bti_ref[j], cre_ref[j], cim_ref[j]
        wbt, wct, ktt = _gen_group_weights(tab, lanes2(btr, bti), lanes2(bti, btr),
                                           lanes2(cre, cre), lanes2(cim, cim), mask_ref[...])
        kw_ref[j, :, :kk] = ktt
        kw_ref[j, :, kk:] = wct
        st = jnp.dot(wbt, ut, preferred_element_type=F32)
        st_ref[j] = st
        s_r, s_i = tiles(st)
        rows8 = jnp.concatenate([tab[0:8], tab[2 * CHUNK:2 * CHUNK + 8]], axis=0)
        cols = jnp.concatenate([rows8] * (LANES // 16), axis=0).T
        ar = jnp.broadcast_to(cols[:n, 0:1], (n, LANES))
        ai = jnp.broadcast_to(cols[:n, 8:9], (n, LANES))
        a_ref[j] = jnp.concatenate([ar, ai], axis=0)
        er, ei = s_r[0], s_i[0]
        for i in range(1, ni):
            mr, mi = cmul(ar, ai, er, ei)
            er, ei = mr + s_r[i], mi + s_i[i]
        ep = jnp.concatenate([er, ei], axis=0).T
        eq = jnp.concatenate([ei, er], axis=0).T
        for b in range(nbat):
            ep_ref[scan_rows(j, b), :] = ep[b * segs:(b + 1) * segs]
            eq_ref[scan_rows(j, b), :] = eq[b * segs:(b + 1) * segs]
        return carry

    lax.fori_loop(0, gb, head, 0, unroll=4)

    c1 = jnp.concatenate([c1_ref[...]] * nbat, axis=0)
    c2 = jnp.concatenate([c2_ref[...]] * nbat, axis=0)
    pairs = gb * nbat

    def seg_step(s, carry):
        p, q = carry
        at = pl.ds(s, pairs, stride=segs)
        hp_ref[at, :] = p
        return (c1 * p + c2 * q + ep_ref[at, :], c1 * q - c2 * p + eq_ref[at, :])

    zero = jnp.zeros((pairs, 2 * n), F32)
    lax.fori_loop(0, segs, seg_step, (zero, zero), unroll=8)

    def tail(j, carry):
        rows = group_rows(j)
        ut = xt_ref[:, rows, :].reshape(kk, nc)
        s_r, s_i = tiles(st_ref[j])
        a = a_ref[j]
        ar, ai = a[:n], a[n:]
        e = jnp.concatenate([hp_ref[scan_rows(j, b), :] for b in range(nbat)], axis=0).T
        hr, hi = e[:n], e[n:]
        h_tiles = []
        for i in range(ni):
            h_tiles.append(jnp.concatenate([hr, hi], axis=0).astype(BF16))
            if i + 1 < ni:
                mr, mi = cmul(ar, ai, hr, hi)
                hr, hi = mr + s_r[i], mi + s_i[i]
        ht = jnp.concatenate(h_tiles, axis=1)
        yt = jnp.dot(kw_ref[j], jnp.concatenate([ut, ht], axis=0), preferred_element_type=F32)
        skip = jnp.concatenate([d_ref[j]] * ni, axis=1)
        y3 = yt.reshape(CHUNK, SSM_P, nc) + skip[None] * ut.astype(F32).reshape(CHUNK, SSM_P, nc)
        ot_ref[:, rows, :] = _gelu_tanh(y3)
        return carry

    lax.fori_loop(0, gb, tail, 0, unroll=4)

    for t in range(CHUNK):
        o_ref[t] = ot_ref[t].T.astype(o_ref.dtype)


def _ssm_mixer(uz, tables, j, nb, ni, gb=8):
    tab, bt_r, bt_i, c_re, c_im, c1, c2, d_tab = tables
    small = pl.BlockSpec((None, gb, SSM_P, SSM_N), lambda i: (j, i, 0, 0))
    row = pl.BlockSpec((None, gb, 2 * SSM_N), lambda i: (j, i, 0))
    t, e2 = uz.shape
    e = e2 // 2
    g = e // SSM_P
    nc = t // CHUNK
    segs = LANES // nb
    uz3 = uz.reshape(CHUNK, nc, e2)
    wl = gb * SSM_P
    kern = functools.partial(_ssm_kernel, gb=gb, nc=nc, ni=ni, segs=segs)
    out = pl.pallas_call(
        kern,
        grid=(g // gb,),
        in_specs=[pl.BlockSpec((CHUNK, nc, wl), lambda i: (0, 0, i)),
                  pl.BlockSpec((None, gb, 2 * CHUNK, 2 * SSM_N), lambda i: (j, i, 0, 0)),
                  small, small, small, small, row, row,
                  pl.BlockSpec((None, gb, SSM_P, LANES), lambda i: (j, i, 0, 0))],
        out_specs=pl.BlockSpec((CHUNK, nc, wl), lambda i: (0, 0, i)),
        out_shape=jax.ShapeDtypeStruct((CHUNK, nc, e), BF16),
        scratch_shapes=[pltpu.VMEM((CHUNK, wl, nc), BF16),
                        pltpu.VMEM((CHUNK, wl, nc), F32),
                        pltpu.VMEM((gb, CHUNK * SSM_P, CHUNK * SSM_P + 2 * SSM_N), BF16),
                        pltpu.VMEM((gb, 2 * SSM_N, nc), F32),
                        pltpu.VMEM((gb, 2 * SSM_N, LANES), F32),
                        pltpu.VMEM((gb * LANES, 2 * SSM_N), F32),
                        pltpu.VMEM((gb * LANES, 2 * SSM_N), F32),
                        pltpu.VMEM((gb * LANES, 2 * SSM_N), F32),
                        pltpu.VMEM((CHUNK * SSM_P, CHUNK * SSM_P), BF16)],
        compiler_params=_cparams(),
        name="ssm_mixer",
    )(uz3, tab, bt_r, bt_i, c_re, c_im, c1, c2, d_tab)
    return out.reshape(t, e)


def _ssm_tables(a_re, a_im, log_dt, b_re, b_im, c_re, c_im, d_skip, seg_tokens):
    dt = jnp.exp(log_dt)[..., None]
    la, th = a_re * dt, a_im * dt
    mag = jnp.exp(la)
    abr = mag * jnp.cos(th)
    abi = mag * jnp.sin(th)
    den = a_re * a_re + a_im * a_im
    nr = abr - 1.0
    fr = (nr * a_re + abi * a_im) / den
    fi = (abi * a_re - nr * a_im) / den
    bbr = fr[..., None] * b_re - fi[..., None] * b_im
    bbi = fr[..., None] * b_im + fi[..., None] * b_re

    def cat(x, y):
        return jnp.concatenate([x, y], axis=-1)

    def power(k, la_, th_):
        m = jnp.exp(k * la_)
        return m * jnp.cos(k * th_), m * jnp.sin(k * th_)

    k = (CHUNK - np.arange(2 * CHUNK)).astype(np.float32)[:, None]
    quarter = np.concatenate([np.full(a_re.shape[-1], np.pi / 2), np.zeros(a_re.shape[-1])])
    tab = (jnp.exp(k * cat(la, la)[:, :, None, :])
           * jnp.sin(k * cat(th, th)[:, :, None, :] + jnp.asarray(quarter, dtype=th.dtype)))
    bt_r, bt_i = jnp.swapaxes(bbr, -1, -2), jnp.swapaxes(bbi, -1, -2)

    sr, si = power(float(seg_tokens), la, th)
    d_tab = jnp.broadcast_to(d_skip.reshape(d_skip.shape[0], -1, SSM_P, 1),
                             (d_skip.shape[0], d_skip.shape[1] // SSM_P, SSM_P, LANES))
    return tab, bt_r, bt_i, c_re, c_im, cat(sr, sr), cat(-si, si), d_tab


def _glu_kernel(a_ref, w_ref, b_ref, g_ref, z_ref, o_ref):
    nsub = o_ref.shape[1] // GLU_SUB

    def matmul(s):
        return jnp.dot(a_ref[...], w_ref[:, s * GLU_SUB:(s + 1) * GLU_SUB],
                       preferred_element_type=F32)

    acc = matmul(0)
    for s in range(nsub):
        cs = slice(s * GLU_SUB, (s + 1) * GLU_SUB)
        nxt = matmul(s + 1) if s + 1 < nsub else None
        gg = g_ref[:, cs].astype(F32)
        zz = z_ref[:, cs].astype(F32)
        den = (1.0 + jnp.exp2(-LOG2E * (acc + b_ref[:, cs]))) * (1.0 + jnp.exp2(-LOG2E * zz))
        o_ref[:, cs] = (gg * zz / den).astype(o_ref.dtype)
        acc = nxt


def _glu(gact, w, b_all, uz, j, tm=1024, tn=1024):
    t, e = gact.shape
    zoff = e // tn
    return pl.pallas_call(
        _glu_kernel,
        grid=(e // tn, t // tm),
        in_specs=[pl.BlockSpec((tm, e), lambda c, r: (r, 0)),
                  pl.BlockSpec((e, tn), lambda c, r: (0, c)),
                  pl.BlockSpec((None, 1, tn), lambda c, r: (j, 0, c)),
                  pl.BlockSpec((tm, tn), lambda c, r: (r, c)),
                  pl.BlockSpec((tm, tn), lambda c, r: (r, c + zoff))],
        out_specs=pl.BlockSpec((tm, tn), lambda c, r: (r, c)),
        out_shape=jax.ShapeDtypeStruct((t, e), BF16),
        compiler_params=_cparams(),
        name="glu",
    )(gact, w, b_all, gact, uz)


def _out_proj_kernel(y_ref, w_ref, x_ref, g_ref, *o_refs, last):
    xn = x_ref[...] + jnp.dot(y_ref[...], w_ref[...], preferred_element_type=F32)
    inv = lax.rsqrt(jnp.mean(xn * xn, axis=-1, keepdims=True) + NORM_EPS)
    hn = xn * inv * g_ref[...]
    if last:
        o_refs[0][...] = hn
    else:
        o_refs[0][...] = xn
        o_refs[1][...] = hn.astype(o_refs[1].dtype)


def _out_proj(y, w, x, gains, layer, last, tm=512):
    t, e = y.shape
    d = w.shape[1]
    row = pl.BlockSpec((tm, d), lambda i: (i, 0))
    if last:
        out_shape = jax.ShapeDtypeStruct((t, d), F32)
        out_specs = row
    else:
        out_shape = (jax.ShapeDtypeStruct((t, d), F32), jax.ShapeDtypeStruct((t, d), BF16))
        out_specs = (row, row)
    return pl.pallas_call(
        functools.partial(_out_proj_kernel, last=last),
        grid=(t // tm,),
        in_specs=[pl.BlockSpec((tm, e), lambda i: (i, 0)),
                  pl.BlockSpec((e, d), lambda i: (0, 0), pipeline_mode=pl.Buffered(1)),
                  row,
                  pl.BlockSpec((None, 1, d), lambda i: (layer, 0, 0))],
        out_specs=out_specs,
        out_shape=out_shape,
        compiler_params=_cparams(),
        name="out_proj",
    )(y, w, x, gains)


def kernel(x, norm_g, final_norm_g, pool_w_in, pool_w_grp, pool_scale, pool_w_out, ssm_w_in, ssm_a_re, ssm_a_im, ssm_log_dt, ssm_b_re, ssm_b_im, ssm_c_re, ssm_c_im, ssm_d, ssm_w_glu, ssm_b_glu, ssm_w_out):
    nb, seq_len, d = x.shape
    depth = norm_g.shape[0]
    segs = LANES // nb
    ni = seq_len // (CHUNK * segs)
    assert nb * segs == LANES and ni * segs * CHUNK == seq_len
    t = nb * seq_len

    gains = jnp.concatenate([norm_g, final_norm_g[None]], axis=0)[:, None, :]
    pool_scale3 = pool_scale[:, None, :]
    ssm_b_glu3 = ssm_b_glu[:, None, :]
    ssm_tables = _ssm_tables(ssm_a_re, ssm_a_im, ssm_log_dt, ssm_b_re, ssm_b_im,
                             ssm_c_re, ssm_c_im, ssm_d, ni * CHUNK)

    xs = x.reshape(nb, segs, ni, CHUNK, d).transpose(3, 2, 0, 1, 4).reshape(t, d)
    h = xs
    for layer in range(depth):
        j = layer // 2
        if layer % 2 == 0:
            uz, (w_out,) = _in_proj(h, gains, layer, pool_w_in, j, [pool_w_out])
            y = _pool_mixer(uz, pool_w_grp, pool_scale3, j, nb, ni)
        else:
            uz, (w_out, w_glu) = _in_proj(h, gains, layer, ssm_w_in, j, [ssm_w_out, ssm_w_glu])
            gact = _ssm_mixer(uz, ssm_tables, j, nb, ni)
            y = _glu(gact, w_glu, ssm_b_glu3, uz, j)
        last = layer == depth - 1
        res = _out_proj(y, w_out, xs, gains, layer + 1, last)
        if last:
            return res.reshape(CHUNK, ni, nb, segs, d).transpose(2, 3, 1, 0, 4).reshape(nb, seq_len, d)
        xs, h = res
```

```python
import functools
import math

import jax
import jax.numpy as jnp
import numpy as np
from jax import lax
from jax.experimental import pallas as pl
from jax.experimental.pallas import tpu as pltpu

F32 = jnp.float32
BF16 = jnp.bfloat16

POOL_WINDOWS = (2, 4, 8, 16)
SSM_P = 16
SSM_N = 64
CHUNK = 16
LANES = 128
GLU_SUB = 256
PACK = 16
LOG2E = math.log2(math.e)
NORM_EPS = 1e-6
VMEM_LIMIT = 56 * 1024 * 1024


def _cparams():
    return pltpu.CompilerParams(vmem_limit_bytes=VMEM_LIMIT)


def _gelu_tanh(y):
    a = -2.0 * math.sqrt(2.0 / math.pi) * math.log2(math.e)
    return y / (1.0 + jnp.exp2(y * (a + (a * 0.044715) * (y * y))))


def _in_proj_kernel(*refs, norm, ncast):
    a_ref, g_ref, w_ref = refs[:3]
    cast_in, o_ref = refs[3:3 + ncast], refs[3 + ncast]
    cast_out, wb_ref = refs[4 + ncast:4 + 2 * ncast], refs[4 + 2 * ncast]

    @pl.when(pl.program_id(1) == 0)
    def _():
        wb_ref[...] = w_ref[...].astype(BF16)

    a = a_ref[...]
    if norm:
        inv = lax.rsqrt(jnp.mean(a * a, axis=-1, keepdims=True) + NORM_EPS)
        a = (a * inv * g_ref[...]).astype(BF16)
    o_ref[...] = jnp.dot(a, wb_ref[...], preferred_element_type=F32).astype(o_ref.dtype)
    for src, dst in zip(cast_in, cast_out):
        dst[...] = src[...].astype(BF16)


def _in_proj(h, gains, layer, w_all, j, later_weights, tm=1024, tn=1024):
    t, k = h.shape
    n = w_all.shape[2]
    nr = t // tm
    steps = (n // tn) * nr
    cast_specs_in, cast_specs_out, cast_shapes = [], [], []
    for w in later_weights:
        rows, cols = w.shape[1], w.shape[2]
        ws = rows // steps
        cast_specs_in.append(pl.BlockSpec((None, ws, cols), lambda c, r: (j, c * nr + r, 0)))
        cast_specs_out.append(pl.BlockSpec((ws, cols), lambda c, r: (c * nr + r, 0)))
        cast_shapes.append(jax.ShapeDtypeStruct((rows, cols), BF16))
    outs = pl.pallas_call(
        functools.partial(_in_proj_kernel, norm=h.dtype == F32, ncast=len(later_weights)),
        grid=(n // tn, nr),
        in_specs=[pl.BlockSpec((tm, k), lambda c, r: (r, 0)),
                  pl.BlockSpec((None, 1, k), lambda c, r: (layer, 0, 0)),
                  pl.BlockSpec((None, k, tn), lambda c, r: (j, 0, c))] + cast_specs_in,
        out_specs=tuple([pl.BlockSpec((tm, tn), lambda c, r: (r, c))] + cast_specs_out),
        out_shape=tuple([jax.ShapeDtypeStruct((t, n), BF16)] + cast_shapes),
        scratch_shapes=[pltpu.VMEM((k, tn), BF16)],
        compiler_params=_cparams(),
        name="in_proj",
    )(h, gains, w_all, *later_weights)
    return outs[0], outs[1:]


def _pool_kernel(u_ref, wrap_ref, z_ref, wg_ref, sc_ref, o_ref, wb_ref, hist_ref, p_ref,
                 *, rows, cg):
    g = pl.program_id(0)
    b = pl.program_id(1)
    i = pl.program_id(2)

    @pl.when((i == 0) & (b == 0))
    def _():
        wb_ref[...] = wg_ref[...].astype(BF16)

    for gi, w in enumerate(POOL_WINDOWS):
        @pl.when(g == gi)
        def _(w=w):
            _pool_step(u_ref, wrap_ref, z_ref, sc_ref, o_ref, wb_ref, hist_ref, p_ref, i,
                       w=w, rows=rows, cg=cg)


def _pool_step(u_ref, wrap_ref, z_ref, sc_ref, o_ref, wb_ref, hist_ref, p_ref, i, *, w, rows, cg):
    @pl.when(i == 0)
    def _():
        seg0 = lax.broadcasted_iota(jnp.int32, (rows, 1), 0) == 0
        for m in range(1, w):
            blk = wrap_ref[CHUNK - m].astype(F32)
            hist_ref[m - 1] = jnp.where(seg0, 0.0, pltpu.roll(blk, 1, axis=0)).astype(BF16)

    def strip(rs, cs, first_tile):
        cur = [u_ref[t, rs, cs].astype(F32) for t in range(CHUNK)]
        s = {t: cur[t] for t in range(CHUNK)}
        for m in range(1, w):
            s[-m] = hist_ref[m - 1, rs, cs].astype(F32)
        lo, k = -(w - 1), 1
        while k < w:
            s = {t: s[t] + s[t - k] for t in range(lo + k, CHUNK)}
            lo += k
            k *= 2
        if first_tile:
            seq_start = (lax.broadcasted_iota(jnp.int32, (PACK, 1), 0) == 0) & (i == 0)
        for t in range(CHUNK):
            inv = 1.0 / w
            if first_tile:
                inv = jnp.where(seq_start, 1.0 / min(t + 1, w), inv)
            p_ref[t, rs, cs] = (s[t] * inv - cur[t]).astype(BF16)

    for r in range(rows // PACK):
        rs = slice(r * PACK, (r + 1) * PACK)
        for c in range(cg // LANES):
            strip(rs, slice(c * LANES, (c + 1) * LANES), r == 0)
        p = p_ref[:, rs, :].reshape(CHUNK * PACK, cg)
        m = jnp.dot(p, wb_ref[...], preferred_element_type=F32)
        zz = z_ref[:, rs, :].reshape(CHUNK * PACK, cg).astype(F32)
        y = m * sc_ref[...] * zz / (1.0 + jnp.exp2(-LOG2E * zz))
        o_ref[:, rs, :] = y.astype(o_ref.dtype).reshape(CHUNK, PACK, cg)

    for m in range(1, w):
        hist_ref[m - 1] = u_ref[CHUNK - m]


def _pool_mixer(uz, w_grp_all, scale_all, j, nb, ni):
    t, e2 = uz.shape
    e = e2 // 2
    ng, cg = w_grp_all.shape[1], w_grp_all.shape[2]
    assert ng == len(POOL_WINDOWS)
    nc = t // CHUNK
    rows = nc // (ni * nb)
    uz3 = uz.reshape(CHUNK, nc, e2)
    blk = (CHUNK, rows, cg)
    out = pl.pallas_call(
        functools.partial(_pool_kernel, rows=rows, cg=cg),
        grid=(ng, nb, ni),
        in_specs=[pl.BlockSpec(blk, lambda g, b, i: (0, i * nb + b, g)),
                  pl.BlockSpec(blk, lambda g, b, i: (0, (ni - 1) * nb + b, g)),
                  pl.BlockSpec(blk, lambda g, b, i: (0, i * nb + b, ng + g)),
                  pl.BlockSpec((None, None, cg, cg), lambda g, b, i: (j, g, 0, 0)),
                  pl.BlockSpec((None, 1, cg), lambda g, b, i: (j, 0, g))],
        out_specs=pl.BlockSpec(blk, lambda g, b, i: (0, i * nb + b, g)),
        out_shape=jax.ShapeDtypeStruct((CHUNK, nc, e), BF16),
        scratch_shapes=[pltpu.VMEM((cg, cg), BF16),
                        pltpu.VMEM((max(POOL_WINDOWS) - 1, rows, cg), BF16),
                        pltpu.VMEM((CHUNK, rows, cg), BF16)],
        compiler_params=_cparams(),
        name="pool_mixer",
    )(uz3, uz3, uz3, w_grp_all, scale_all)
    return out.reshape(t, e)


def _causal_mask():
    n = CHUNK * SSM_P
    r_tau = lax.broadcasted_iota(jnp.int32, (n, n), 0) // SSM_P
    c_tau = lax.broadcasted_iota(jnp.int32, (n, n), 1) // SSM_P
    return jnp.where(r_tau >= c_tau, 1.0, 0.0).astype(BF16)


def _gen_group_weights(tab, bp, bq, cr, ci, mask):
    def re(k):
        return jnp.broadcast_to(tab[CHUNK - k:CHUNK - k + 1, :], (SSM_P, 2 * SSM_N))

    def im(k):
        return jnp.broadcast_to(tab[3 * CHUNK - k:3 * CHUNK - k + 1, :], (SSM_P, 2 * SSM_N))

    low = lax.broadcasted_iota(jnp.int32, (SSM_P, 2 * SSM_N), 1) < SSM_N
    bqs = jnp.where(low, -bq, bq)
    ca = jnp.where(low, cr, -ci)
    cb_ = jnp.where(low, -ci, -cr)
    wb = jnp.concatenate([bp * re(CHUNK - 1 - t) + bqs * im(CHUNK - 1 - t) for t in range(CHUNK)], axis=0)
    rn = jnp.concatenate([bp * re(-t) + bqs * im(-t) for t in range(CHUNK)], axis=0)
    cb = [ca * re(d) + cb_ * im(d) for d in range(CHUNK + 1)]
    lm = jnp.concatenate(cb[:CHUNK], axis=0)
    wc = jnp.concatenate(cb[1:], axis=0)
    lm_h, rn_h = lm.astype(BF16), rn.astype(BF16)
    lm_l = (lm - lm_h.astype(F32)).astype(BF16)
    rn_l = (rn - rn_h.astype(F32)).astype(BF16)
    kt = lax.dot_general(jnp.concatenate([lm_h, lm_h, lm_l], axis=1),
                         jnp.concatenate([rn_h, rn_l, rn_h], axis=1),
                         (((1,), (1,)), ((), ())), preferred_element_type=F32)
    return wb.T.astype(BF16), wc.astype(BF16), kt.astype(BF16) * mask


def _ssm_kernel(u_ref, tab_ref, btr_ref, bti_ref, cre_ref, cim_ref, c1_ref, c2_ref, d_ref, o_ref,
                xt_ref, ot_ref, kw_ref, st_ref, a_ref, ep_ref, eq_ref, hp_ref, mask_ref,
                *, gb, nc, ni, segs):
    n = SSM_N
    kk = CHUNK * SSM_P
    nbat = LANES // segs

    @pl.when(pl.program_id(0) == 0)
    def _():
        mask_ref[...] = _causal_mask()

    for t in range(CHUNK):
        xt_ref[t] = u_ref[t].T

    def cmul(ar, ai, xr, xi):
        return ar * xr - ai * xi, ar * xi + ai * xr

    def group_rows(j):
        return pl.ds(pl.multiple_of(j * SSM_P, SSM_P), SSM_P)

    def scan_rows(j, b):
        return pl.ds(pl.multiple_of((b * gb + j) * segs, segs), segs)

    def lanes2(x, y):
        return jnp.concatenate([x, y], axis=1)

    def tiles(st):
        return ([st[:n, i * LANES:(i + 1) * LANES] for i in range(ni)],
                [st[n:, i * LANES:(i + 1) * LANES] for i in range(ni)])

    def head(j, carry):
        ut = xt_ref[:, group_rows(j), :].reshape(kk, nc)
        t = tab_ref[j]
        tsw = pltpu.roll(t, n, axis=1)
        low = lax.broadcasted_iota(jnp.int32, t.shape, 1) < n
        tab = jnp.concatenate([jnp.where(low, t, tsw), jnp.where(low, tsw, t)], axis=0)
        btr, bti, cre, cim = btr_ref[j], bti_ref[j], cre_ref[j], cim_ref[j]
        wbt, wct, ktt = _gen_group_weights(tab, lanes2(btr, bti), lanes2(bti, btr),
                                           lanes2(cre, cre), lanes2(cim, cim), mask_ref[...])
        kw_ref[j, :, :kk] = ktt
        kw_ref[j, :, kk:] = wct
        st = jnp.dot(wbt, ut, preferred_element_type=F32)
        st_ref[j] = st
        s_r, s_i = tiles(st)
        rows8 = jnp.concatenate([tab[0:8], tab[2 * CHUNK:2 * CHUNK + 8]], axis=0)
        cols = jnp.concatenate([rows8] * (LANES // 16), axis=0).T
        ar = jnp.broadcast_to(cols[:n, 0:1], (n, LANES))
        ai = jnp.broadcast_to(cols[:n, 8:9], (n, LANES))
        a_ref[j] = jnp.concatenate([ar, ai], axis=0)
        er, ei = s_r[0], s_i[0]
        for i in range(1, ni):
            mr, mi = cmul(ar, ai, er, ei)
            er, ei = mr + s_r[i], mi + s_i[i]
        ep = jnp.concatenate([er, ei], axis=0).T
        eq = jnp.concatenate([ei, er], axis=0).T
        for b in range(nbat):
            ep_ref[scan_rows(j, b), :] = ep[b * segs:(b + 1) * segs]
            eq_ref[scan_rows(j, b), :] = eq[b * segs:(b + 1) * segs]
        return carry

    lax.fori_loop(0, gb, head, 0, unroll=4)

    c1 = jnp.concatenate([c1_ref[...]] * nbat, axis=0)
    c2 = jnp.concatenate([c2_ref[...]] * nbat, axis=0)
    pairs = gb * nbat

    def seg_step(s, carry):
        p, q = carry
        at = pl.ds(s, pairs, stride=segs)
        hp_ref[at, :] = p
        return (c1 * p + c2 * q + ep_ref[at, :], c1 * q - c2 * p + eq_ref[at, :])

    zero = jnp.zeros((pairs, 2 * n), F32)
    lax.fori_loop(0, segs, seg_step, (zero, zero), unroll=8)

    def tail(j, carry):
        rows = group_rows(j)
        ut = xt_ref[:, rows, :].reshape(kk, nc)
        s_r, s_i = tiles(st_ref[j])
        a = a_ref[j]
        ar, ai = a[:n], a[n:]
        e = jnp.concatenate([hp_ref[scan_rows(j, b), :] for b in range(nbat)], axis=0).T
        hr, hi = e[:n], e[n:]
        h_tiles = []
        for i in range(ni):
            h_tiles.append(jnp.concatenate([hr, hi], axis=0).astype(BF16))
            if i + 1 < ni:
                mr, mi = cmul(ar, ai, hr, hi)
                hr, hi = mr + s_r[i], mi + s_i[i]
        ht = jnp.concatenate(h_tiles, axis=1)
        yt = jnp.dot(kw_ref[j], jnp.concatenate([ut, ht], axis=0), preferred_element_type=F32)
        skip = jnp.concatenate([d_ref[j]] * ni, axis=1)
        y3 = yt.reshape(CHUNK, SSM_P, nc) + skip[None] * ut.astype(F32).reshape(CHUNK, SSM_P, nc)
        ot_ref[:, rows, :] = _gelu_tanh(y3)
        return carry

    lax.fori_loop(0, gb, tail, 0, unroll=4)

    for t in range(CHUNK):
        o_ref[t] = ot_ref[t].T.astype(o_ref.dtype)


def _ssm_mixer(uz, tables, j, nb, ni, gb=16):
    tab, bt_r, bt_i, c_re, c_im, c1, c2, d_tab = tables
    small = pl.BlockSpec((None, gb, SSM_P, SSM_N), lambda i: (j, i, 0, 0))
    row = pl.BlockSpec((None, gb, 2 * SSM_N), lambda i: (j, i, 0))
    t, e2 = uz.shape
    e = e2 // 2
    g = e // SSM_P
    nc = t // CHUNK
    segs = LANES // nb
    uz3 = uz.reshape(CHUNK, nc, e2)
    wl = gb * SSM_P
    kern = functools.partial(_ssm_kernel, gb=gb, nc=nc, ni=ni, segs=segs)
    out = pl.pallas_call(
        kern,
        grid=(g // gb,),
        in_specs=[pl.BlockSpec((CHUNK, nc, wl), lambda i: (0, 0, i)),
                  pl.BlockSpec((None, gb, 2 * CHUNK, 2 * SSM_N), lambda i: (j, i, 0, 0)),
                  small, small, small, small, row, row,
                  pl.BlockSpec((None, gb, SSM_P, LANES), lambda i: (j, i, 0, 0))],
        out_specs=pl.BlockSpec((CHUNK, nc, wl), lambda i: (0, 0, i)),
        out_shape=jax.ShapeDtypeStruct((CHUNK, nc, e), BF16),
        scratch_shapes=[pltpu.VMEM((CHUNK, wl, nc), BF16),
                        pltpu.VMEM((CHUNK, wl, nc), F32),
                        pltpu.VMEM((gb, CHUNK * SSM_P, CHUNK * SSM_P + 2 * SSM_N), BF16),
                        pltpu.VMEM((gb, 2 * SSM_N, nc), F32),
                        pltpu.VMEM((gb, 2 * SSM_N, LANES), F32),
                        pltpu.VMEM((gb * LANES, 2 * SSM_N), F32),
                        pltpu.VMEM((gb * LANES, 2 * SSM_N), F32),
                        pltpu.VMEM((gb * LANES, 2 * SSM_N), F32),
                        pltpu.VMEM((CHUNK * SSM_P, CHUNK * SSM_P), BF16)],
        compiler_params=_cparams(),
        name="ssm_mixer",
    )(uz3, tab, bt_r, bt_i, c_re, c_im, c1, c2, d_tab)
    return out.reshape(t, e)


def _ssm_tables(a_re, a_im, log_dt, b_re, b_im, c_re, c_im, d_skip, seg_tokens):
    dt = jnp.exp(log_dt)[..., None]
    la, th = a_re * dt, a_im * dt
    mag = jnp.exp(la)
    abr = mag * jnp.cos(th)
    abi = mag * jnp.sin(th)
    den = a_re * a_re + a_im * a_im
    nr = abr - 1.0
    fr = (nr * a_re + abi * a_im) / den
    fi = (abi * a_re - nr * a_im) / den
    bbr = fr[..., None] * b_re - fi[..., None] * b_im
    bbi = fr[..., None] * b_im + fi[..., None] * b_re

    def cat(x, y):
        return jnp.concatenate([x, y], axis=-1)

    def power(k, la_, th_):
        m = jnp.exp(k * la_)
        return m * jnp.cos(k * th_), m * jnp.sin(k * th_)

    k = (CHUNK - np.arange(2 * CHUNK)).astype(np.float32)[:, None]
    quarter = np.concatenate([np.full(a_re.shape[-1], np.pi / 2), np.zeros(a_re.shape[-1])])
    tab = (jnp.exp(k * cat(la, la)[:, :, None, :])
           * jnp.sin(k * cat(th, th)[:, :, None, :] + jnp.asarray(quarter, dtype=th.dtype)))
    bt_r, bt_i = jnp.swapaxes(bbr, -1, -2), jnp.swapaxes(bbi, -1, -2)

    sr, si = power(float(seg_tokens), la, th)
    d_tab = jnp.broadcast_to(d_skip.reshape(d_skip.shape[0], -1, SSM_P, 1),
                             (d_skip.shape[0], d_skip.shape[1] // SSM_P, SSM_P, LANES))
    return tab, bt_r, bt_i, c_re, c_im, cat(sr, sr), cat(-si, si), d_tab


def _glu_kernel(a_ref, w_ref, b_ref, g_ref, z_ref, o_ref):
    nsub = o_ref.shape[1] // GLU_SUB

    def matmul(s):
        return jnp.dot(a_ref[...], w_ref[:, s * GLU_SUB:(s + 1) * GLU_SUB],
                       preferred_element_type=F32)

    acc = matmul(0)
    for s in range(nsub):
        cs = slice(s * GLU_SUB, (s + 1) * GLU_SUB)
        nxt = matmul(s + 1) if s + 1 < nsub else None
        gg = g_ref[:, cs].astype(F32)
        zz = z_ref[:, cs].astype(F32)
        den = (1.0 + jnp.exp2(-LOG2E * (acc + b_ref[:, cs]))) * (1.0 + jnp.exp2(-LOG2E * zz))
        o_ref[:, cs] = (gg * zz / den).astype(o_ref.dtype)
        acc = nxt


def _glu(gact, w, b_all, uz, j, tm=1024, tn=1024):
    t, e = gact.shape
    zoff = e // tn
    return pl.pallas_call(
        _glu_kernel,
        grid=(e // tn, t // tm),
        in_specs=[pl.BlockSpec((tm, e), lambda c, r: (r, 0)),
                  pl.BlockSpec((e, tn), lambda c, r: (0, c)),
                  pl.BlockSpec((None, 1, tn), lambda c, r: (j, 0, c)),
                  pl.BlockSpec((tm, tn), lambda c, r: (r, c)),
                  pl.BlockSpec((tm, tn), lambda c, r: (r, c + zoff))],
        out_specs=pl.BlockSpec((tm, tn), lambda c, r: (r, c)),
        out_shape=jax.ShapeDtypeStruct((t, e), BF16),
        compiler_params=_cparams(),
        name="glu",
    )(gact, w, b_all, gact, uz)


def _out_proj_kernel(y_ref, w_ref, x_ref, g_ref, *o_refs, last):
    xn = x_ref[...] + jnp.dot(y_ref[...], w_ref[...], preferred_element_type=F32)
    inv = lax.rsqrt(jnp.mean(xn * xn, axis=-1, keepdims=True) + NORM_EPS)
    hn = xn * inv * g_ref[...]
    if last:
        o_refs[0][...] = hn
    else:
        o_refs[0][...] = xn
        o_refs[1][...] = hn.astype(o_refs[1].dtype)


def _out_proj(y, w, x, gains, layer, last, tm=512):
    t, e = y.shape
    d = w.shape[1]
    row = pl.BlockSpec((tm, d), lambda i: (i, 0))
    if last:
        out_shape = jax.ShapeDtypeStruct((t, d), F32)
        out_specs = row
    else:
        out_shape = (jax.ShapeDtypeStruct((t, d), F32), jax.ShapeDtypeStruct((t, d), BF16))
        out_specs = (row, row)
    return pl.pallas_call(
        functools.partial(_out_proj_kernel, last=last),
        grid=(t // tm,),
        in_specs=[pl.BlockSpec((tm, e), lambda i: (i, 0)),
                  pl.BlockSpec((e, d), lambda i: (0, 0), pipeline_mode=pl.Buffered(1)),
                  row,
                  pl.BlockSpec((None, 1, d), lambda i: (layer, 0, 0))],
        out_specs=out_specs,
        out_shape=out_shape,
        compiler_params=_cparams(),
        name="out_proj",
    )(y, w, x, gains)


def kernel(x, norm_g, final_norm_g, pool_w_in, pool_w_grp, pool_scale, pool_w_out, ssm_w_in, ssm_a_re, ssm_a_im, ssm_log_dt, ssm_b_re, ssm_b_im, ssm_c_re, ssm_c_im, ssm_d, ssm_w_glu, ssm_b_glu, ssm_w_out):
    nb, seq_len, d = x.shape
    depth = norm_g.shape[0]
    segs = LANES // nb
    ni = seq_len // (CHUNK * segs)
    assert nb * segs == LANES and ni * segs * CHUNK == seq_len
    t = nb * seq_len

    gains = jnp.concatenate([norm_g, final_norm_g[None]], axis=0)[:, None, :]
    pool_scale3 = pool_scale[:, None, :]
    ssm_b_glu3 = ssm_b_glu[:, None, :]
    ssm_tables = _ssm_tables(ssm_a_re, ssm_a_im, ssm_log_dt, ssm_b_re, ssm_b_im,
                             ssm_c_re, ssm_c_im, ssm_d, ni * CHUNK)

    xs = x.reshape(nb, segs, ni, CHUNK, d).transpose(3, 2, 0, 1, 4).reshape(t, d)
    h = xs
    for layer in range(depth):
        j = layer // 2
        if layer % 2 == 0:
            uz, (w_out,) = _in_proj(h, gains, layer, pool_w_in, j, [pool_w_out])
            y = _pool_mixer(uz, pool_w_grp, pool_scale3, j, nb, ni)
        else:
            uz, (w_out, w_glu) = _in_proj(h, gains, layer, ssm_w_in, j, [ssm_w_out, ssm_w_glu])
            gact = _ssm_mixer(uz, ssm_tables, j, nb, ni)
            y = _glu(gact, w_glu, ssm_b_glu3, uz, j)
        last = layer == depth - 1
        res = _out_proj(y, w_out, xs, gains, layer + 1, last)
        if last:
            return res.reshape(CHUNK, ni, nb, segs, d).transpose(2, 3, 1, 0, 4).reshape(nb, seq_len, d)
        xs, h = res
```

```python
import functools
import math

import jax
import jax.numpy as jnp
import numpy as np
from jax import lax
from jax.experimental import pallas as pl
from jax.experimental.pallas import tpu as pltpu

F32 = jnp.float32
BF16 = jnp.bfloat16

POOL_WINDOWS = (2, 4, 8, 16)
SSM_P = 16
SSM_N = 64
CHUNK = 16
LANES = 128
GLU_SUB = 256
PACK = 16
LOG2E = math.log2(math.e)
NORM_EPS = 1e-6
VMEM_LIMIT = 56 * 1024 * 1024


def _cparams():
    return pltpu.CompilerParams(vmem_limit_bytes=VMEM_LIMIT)


def _gelu_tanh(y):
    a = -2.0 * math.sqrt(2.0 / math.pi) * math.log2(math.e)
    return y / (1.0 + jnp.exp2(y * (a + (a * 0.044715) * (y * y))))


def _in_proj_kernel(*refs, norm, ncast):
    a_ref, g_ref, w_ref = refs[:3]
    cast_in, o_ref = refs[3:3 + ncast], refs[3 + ncast]
    cast_out, wb_ref = refs[4 + ncast:4 + 2 * ncast], refs[4 + 2 * ncast]

    @pl.when(pl.program_id(1) == 0)
    def _():
        wb_ref[...] = w_ref[...].astype(BF16)

    a = a_ref[...]
    if norm:
        inv = lax.rsqrt(jnp.mean(a * a, axis=-1, keepdims=True) + NORM_EPS)
        a = (a * inv * g_ref[...]).astype(BF16)
    o_ref[...] = jnp.dot(a, wb_ref[...], preferred_element_type=F32).astype(o_ref.dtype)
    for src, dst in zip(cast_in, cast_out):
        dst[...] = src[...].astype(BF16)


def _in_proj(h, gains, layer, w_all, j, later_weights, tm=1024, tn=1024):
    t, k = h.shape
    n = w_all.shape[2]
    nr = t // tm
    steps = (n // tn) * nr
    cast_specs_in, cast_specs_out, cast_shapes = [], [], []
    for w in later_weights:
        rows, cols = w.shape[1], w.shape[2]
        ws = rows // steps
        cast_specs_in.append(pl.BlockSpec((None, ws, cols), lambda c, r: (j, c * nr + r, 0)))
        cast_specs_out.append(pl.BlockSpec((ws, cols), lambda c, r: (c * nr + r, 0)))
        cast_shapes.append(jax.ShapeDtypeStruct((rows, cols), BF16))
    outs = pl.pallas_call(
        functools.partial(_in_proj_kernel, norm=h.dtype == F32, ncast=len(later_weights)),
        grid=(n // tn, nr),
        in_specs=[pl.BlockSpec((tm, k), lambda c, r: (r, 0)),
                  pl.BlockSpec((None, 1, k), lambda c, r: (layer, 0, 0)),
                  pl.BlockSpec((None, k, tn), lambda c, r: (j, 0, c))] + cast_specs_in,
        out_specs=tuple([pl.BlockSpec((tm, tn), lambda c, r: (r, c))] + cast_specs_out),
        out_shape=tuple([jax.ShapeDtypeStruct((t, n), BF16)] + cast_shapes),
        scratch_shapes=[pltpu.VMEM((k, tn), BF16)],
        compiler_params=_cparams(),
        name="in_proj",
    )(h, gains, w_all, *later_weights)
    return outs[0], outs[1:]


def _pool_kernel(u_ref, wrap_ref, z_ref, wg_ref, sc_ref, o_ref, wb_ref, hist_ref, p_ref,
                 *, rows, cg):
    g = pl.program_id(0)
    b = pl.program_id(1)
    i = pl.program_id(2)

    @pl.when((i == 0) & (b == 0))
    def _():
        wb_ref[...] = wg_ref[...].astype(BF16)

    for gi, w in enumerate(POOL_WINDOWS):
        @pl.when(g == gi)
        def _(w=w):
            _pool_step(u_ref, wrap_ref, z_ref, sc_ref, o_ref, wb_ref, hist_ref, p_ref, i,
                       w=w, rows=rows, cg=cg)


def _pool_step(u_ref, wrap_ref, z_ref, sc_ref, o_ref, wb_ref, hist_ref, p_ref, i, *, w, rows, cg):
    @pl.when(i == 0)
    def _():
        seg0 = lax.broadcasted_iota(jnp.int32, (rows, 1), 0) == 0
        for m in range(1, w):
            blk = wrap_ref[CHUNK - m].astype(F32)
            hist_ref[m - 1] = jnp.where(seg0, 0.0, pltpu.roll(blk, 1, axis=0)).astype(BF16)

    def strip(rs, cs, first_tile):
        cur = [u_ref[t, rs, cs].astype(F32) for t in range(CHUNK)]
        s = {t: cur[t] for t in range(CHUNK)}
        for m in range(1, w):
            s[-m] = hist_ref[m - 1, rs, cs].astype(F32)
        lo, k = -(w - 1), 1
        while k < w:
            s = {t: s[t] + s[t - k] for t in range(lo + k, CHUNK)}
            lo += k
            k *= 2
        if first_tile:
            seq_start = (lax.broadcasted_iota(jnp.int32, (PACK, 1), 0) == 0) & (i == 0)
        for t in range(CHUNK):
            inv = 1.0 / w
            if first_tile:
                inv = jnp.where(seq_start, 1.0 / min(t + 1, w), inv)
            p_ref[t, rs, cs] = (s[t] * inv - cur[t]).astype(BF16)

    for r in range(rows // PACK):
        rs = slice(r * PACK, (r + 1) * PACK)
        for c in range(cg // LANES):
            strip(rs, slice(c * LANES, (c + 1) * LANES), r == 0)
        p = p_ref[:, rs, :].reshape(CHUNK * PACK, cg)
        m = jnp.dot(p, wb_ref[...], preferred_element_type=F32)
        zz = z_ref[:, rs, :].reshape(CHUNK * PACK, cg).astype(F32)
        y = m * sc_ref[...] * zz / (1.0 + jnp.exp2(-LOG2E * zz))
        o_ref[:, rs, :] = y.astype(o_ref.dtype).reshape(CHUNK, PACK, cg)

    for m in range(1, w):
        hist_ref[m - 1] = u_ref[CHUNK - m]


def _pool_mixer(uz, w_grp_all, scale_all, j, nb, ni):
    t, e2 = uz.shape
    e = e2 // 2
    ng, cg = w_grp_all.shape[1], w_grp_all.shape[2]
    assert ng == len(POOL_WINDOWS)
    nc = t // CHUNK
    rows = nc // (ni * nb)
    uz3 = uz.reshape(CHUNK, nc, e2)
    blk = (CHUNK, rows, cg)
    out = pl.pallas_call(
        functools.partial(_pool_kernel, rows=rows, cg=cg),
        grid=(ng, nb, ni),
        in_specs=[pl.BlockSpec(blk, lambda g, b, i: (0, i * nb + b, g)),
                  pl.BlockSpec(blk, lambda g, b, i: (0, (ni - 1) * nb + b, g)),
                  pl.BlockSpec(blk, lambda g, b, i: (0, i * nb + b, ng + g)),
                  pl.BlockSpec((None, None, cg, cg), lambda g, b, i: (j, g, 0, 0)),
                  pl.BlockSpec((None, 1, cg), lambda g, b, i: (j, 0, g))],
        out_specs=pl.BlockSpec(blk, lambda g, b, i: (0, i * nb + b, g)),
        out_shape=jax.ShapeDtypeStruct((CHUNK, nc, e), BF16),
        scratch_shapes=[pltpu.VMEM((cg, cg), BF16),
                        pltpu.VMEM((max(POOL_WINDOWS) - 1, rows, cg), BF16),
                        pltpu.VMEM((CHUNK, rows, cg), BF16)],
        compiler_params=_cparams(),
        name="pool_mixer",
    )(uz3, uz3, uz3, w_grp_all, scale_all)
    return out.reshape(t, e)


def _causal_mask():
    n = CHUNK * SSM_P
    r_tau = lax.broadcasted_iota(jnp.int32, (n, n), 0) // SSM_P
    c_tau = lax.broadcasted_iota(jnp.int32, (n, n), 1) // SSM_P
    return jnp.where(r_tau >= c_tau, 1.0, 0.0).astype(BF16)


def _gen_group_weights(tab, bp, bq, cr, ci, mask):
    def re(k):
        return jnp.broadcast_to(tab[CHUNK - k:CHUNK - k + 1, :], (SSM_P, 2 * SSM_N))

    def im(k):
        return jnp.broadcast_to(tab[3 * CHUNK - k:3 * CHUNK - k + 1, :], (SSM_P, 2 * SSM_N))

    low = lax.broadcasted_iota(jnp.int32, (SSM_P, 2 * SSM_N), 1) < SSM_N
    bqs = jnp.where(low, -bq, bq)
    ca = jnp.where(low, cr, -ci)
    cb_ = jnp.where(low, -ci, -cr)
    wb = jnp.concatenate([bp * re(CHUNK - 1 - t) + bqs * im(CHUNK - 1 - t) for t in range(CHUNK)], axis=0)
    rn = jnp.concatenate([bp * re(-t) + bqs * im(-t) for t in range(CHUNK)], axis=0)
    cb = [ca * re(d) + cb_ * im(d) for d in range(CHUNK + 1)]
    lm = jnp.concatenate(cb[:CHUNK], axis=0)
    wc = jnp.concatenate(cb[1:], axis=0)
    lm_h, rn_h = lm.astype(BF16), rn.astype(BF16)
    lm_l = (lm - lm_h.astype(F32)).astype(BF16)
    rn_l = (rn - rn_h.astype(F32)).astype(BF16)
    kt = lax.dot_general(jnp.concatenate([lm_h, lm_h, lm_l], axis=1),
                         jnp.concatenate([rn_h, rn_l, rn_h], axis=1),
                         (((1,), (1,)), ((), ())), preferred_element_type=F32)
    return wb.T.astype(BF16), wc.astype(BF16), kt.astype(BF16) * mask


def _ssm_kernel(u_ref, tab_ref, btr_ref, bti_ref, cre_ref, cim_ref, c1_ref, c2_ref, d_ref, o_ref,
                xt_ref, ot_ref, kw_ref, st_ref, a_ref, ep_ref, hp_ref, mask_ref,
                *, gb, nc, ni, segs):
    n = SSM_N
    kk = CHUNK * SSM_P
    nbat = LANES // segs

    @pl.when(pl.program_id(0) == 0)
    def _():
        mask_ref[...] = _causal_mask()

    for t in range(CHUNK):
        xt_ref[t] = u_ref[t].T

    def cmul(ar, ai, xr, xi):
        return ar * xr - ai * xi, ar * xi + ai * xr

    def group_rows(j):
        return pl.ds(pl.multiple_of(j * SSM_P, SSM_P), SSM_P)

    def scan_rows(j, b):
        return pl.ds(pl.multiple_of((b * gb + j) * segs, segs), segs)

    def lanes2(x, y):
        return jnp.concatenate([x, y], axis=1)

    def tiles(st):
        return ([st[:n, i * LANES:(i + 1) * LANES] for i in range(ni)],
                [st[n:, i * LANES:(i + 1) * LANES] for i in range(ni)])

    def head(j, carry):
        ut = xt_ref[:, group_rows(j), :].reshape(kk, nc)
        t = tab_ref[j]
        tsw = pltpu.roll(t, n, axis=1)
        low = lax.broadcasted_iota(jnp.int32, t.shape, 1) < n
        tab = jnp.concatenate([jnp.where(low, t, tsw), jnp.where(low, tsw, t)], axis=0)
        btr, bti, cre, cim = btr_ref[j], bti_ref[j], cre_ref[j], cim_ref[j]
        wbt, wct, ktt = _gen_group_weights(tab, lanes2(btr, bti), lanes2(bti, btr),
                                           lanes2(cre, cre), lanes2(cim, cim), mask_ref[...])
        kw_ref[j, :, :kk] = ktt
        kw_ref[j, :, kk:] = wct
        st = jnp.dot(wbt, ut, preferred_element_type=F32)
        st_ref[j] = st
        s_r, s_i = tiles(st)
        rows8 = jnp.concatenate([tab[0:8], tab[2 * CHUNK:2 * CHUNK + 8]], axis=0)
        cols = jnp.concatenate([rows8] * (LANES // 16), axis=0).T
        ar = jnp.broadcast_to(cols[:n, 0:1], (n, LANES))
        ai = jnp.broadcast_to(cols[:n, 8:9], (n, LANES))
        a_ref[j] = jnp.concatenate([ar, ai], axis=0)
        er, ei = s_r[0], s_i[0]
        for i in range(1, ni):
            mr, mi = cmul(ar, ai, er, ei)
            er, ei = mr + s_r[i], mi + s_i[i]
        ep = jnp.concatenate([er, ei], axis=0).T
        for b in range(nbat):
            ep_ref[scan_rows(j, b), :] = ep[b * segs:(b + 1) * segs]
        return carry

    lax.fori_loop(0, gb, head, 0, unroll=4)

    c1 = jnp.concatenate([c1_ref[...]] * nbat, axis=0)
    c2 = jnp.concatenate([c2_ref[...]] * nbat, axis=0)
    pairs = gb * nbat

    def seg_step(s, carry):
        p, q = carry
        at = pl.ds(s, pairs, stride=segs)
        hp_ref[at, :] = p
        e = ep_ref[at, :]
        return (c1 * p + c2 * q + e, c1 * q - c2 * p + pltpu.roll(e, n, axis=1))

    zero = jnp.zeros((pairs, 2 * n), F32)
    lax.fori_loop(0, segs, seg_step, (zero, zero), unroll=8)

    def tail(j, carry):
        rows = group_rows(j)
        ut = xt_ref[:, rows, :].reshape(kk, nc)
        s_r, s_i = tiles(st_ref[j])
        a = a_ref[j]
        ar, ai = a[:n], a[n:]
        e = jnp.concatenate([hp_ref[scan_rows(j, b), :] for b in range(nbat)], axis=0).T
        hr, hi = e[:n], e[n:]
        h_tiles = []
        for i in range(ni):
            h_tiles.append(jnp.concatenate([hr, hi], axis=0).astype(BF16))
            if i + 1 < ni:
                mr, mi = cmul(ar, ai, hr, hi)
                hr, hi = mr + s_r[i], mi + s_i[i]
        ht = jnp.concatenate(h_tiles, axis=1)
        yt = jnp.dot(kw_ref[j], jnp.concatenate([ut, ht], axis=0), preferred_element_type=F32)
        skip = jnp.concatenate([d_ref[j]] * ni, axis=1)
        y3 = yt.reshape(CHUNK, SSM_P, nc) + skip[None] * ut.astype(F32).reshape(CHUNK, SSM_P, nc)
        ot_ref[:, rows, :] = _gelu_tanh(y3)
        return carry

    lax.fori_loop(0, gb, tail, 0, unroll=4)

    for t in range(CHUNK):
        o_ref[t] = ot_ref[t].T.astype(o_ref.dtype)


def _ssm_mixer(uz, tables, j, nb, ni, gb=16):
    tab, bt_r, bt_i, c_re, c_im, c1, c2, d_tab = tables
    small = pl.BlockSpec((None, gb, SSM_P, SSM_N), lambda i: (j, i, 0, 0))
    row = pl.BlockSpec((None, gb, 2 * SSM_N), lambda i: (j, i, 0))
    t, e2 = uz.shape
    e = e2 // 2
    g = e // SSM_P
    nc = t // CHUNK
    segs = LANES // nb
    uz3 = uz.reshape(CHUNK, nc, e2)
    wl = gb * SSM_P
    kern = functools.partial(_ssm_kernel, gb=gb, nc=nc, ni=ni, segs=segs)
    out = pl.pallas_call(
        kern,
        grid=(g // gb,),
        in_specs=[pl.BlockSpec((CHUNK, nc, wl), lambda i: (0, 0, i)),
                  pl.BlockSpec((None, gb, 2 * CHUNK, 2 * SSM_N), lambda i: (j, i, 0, 0)),
                  small, small, small, small, row, row,
                  pl.BlockSpec((None, gb, SSM_P, LANES), lambda i: (j, i, 0, 0))],
        out_specs=pl.BlockSpec((CHUNK, nc, wl), lambda i: (0, 0, i)),
        out_shape=jax.ShapeDtypeStruct((CHUNK, nc, e), BF16),
        scratch_shapes=[pltpu.VMEM((CHUNK, wl, nc), BF16),
                        pltpu.VMEM((CHUNK, wl, nc), F32),
                        pltpu.VMEM((gb, CHUNK * SSM_P, CHUNK * SSM_P + 2 * SSM_N), BF16),
                        pltpu.VMEM((gb, 2 * SSM_N, nc), F32),
                        pltpu.VMEM((gb, 2 * SSM_N, LANES), F32),
                        pltpu.VMEM((gb * LANES, 2 * SSM_N), F32),
                        pltpu.VMEM((gb * LANES, 2 * SSM_N), F32),
                        pltpu.VMEM((CHUNK * SSM_P, CHUNK * SSM_P), BF16)],
        compiler_params=_cparams(),
        name="ssm_mixer",
    )(uz3, tab, bt_r, bt_i, c_re, c_im, c1, c2, d_tab)
    return out.reshape(t, e)


def _ssm_tables(a_re, a_im, log_dt, b_re, b_im, c_re, c_im, d_skip, seg_tokens):
    dt = jnp.exp(log_dt)[..., None]
    la, th = a_re * dt, a_im * dt
    mag = jnp.exp(la)
    abr = mag * jnp.cos(th)
    abi = mag * jnp.sin(th)
    den = a_re * a_re + a_im * a_im
    nr = abr - 1.0
    fr = (nr * a_re + abi * a_im) / den
    fi = (abi * a_re - nr * a_im) / den
    bbr = fr[..., None] * b_re - fi[..., None] * b_im
    bbi = fr[..., None] * b_im + fi[..., None] * b_re

    def cat(x, y):
        return jnp.concatenate([x, y], axis=-1)

    def power(k, la_, th_):
        m = jnp.exp(k * la_)
        return m * jnp.cos(k * th_), m * jnp.sin(k * th_)

    k = (CHUNK - np.arange(2 * CHUNK)).astype(np.float32)[:, None]
    quarter = np.concatenate([np.full(a_re.shape[-1], np.pi / 2), np.zeros(a_re.shape[-1])])
    tab = (jnp.exp(k * cat(la, la)[:, :, None, :])
           * jnp.sin(k * cat(th, th)[:, :, None, :] + jnp.asarray(quarter, dtype=th.dtype)))
    bt_r, bt_i = jnp.swapaxes(bbr, -1, -2), jnp.swapaxes(bbi, -1, -2)

    sr, si = power(float(seg_tokens), la, th)
    d_tab = jnp.broadcast_to(d_skip.reshape(d_skip.shape[0], -1, SSM_P, 1),
                             (d_skip.shape[0], d_skip.shape[1] // SSM_P, SSM_P, LANES))
    return tab, bt_r, bt_i, c_re, c_im, cat(sr, sr), cat(-si, si), d_tab


def _glu_kernel(a_ref, w_ref, b_ref, g_ref, z_ref, o_ref):
    nsub = o_ref.shape[1] // GLU_SUB

    def matmul(s):
        return jnp.dot(a_ref[...], w_ref[:, s * GLU_SUB:(s + 1) * GLU_SUB],
                       preferred_element_type=F32)

    acc = matmul(0)
    for s in range(nsub):
        cs = slice(s * GLU_SUB, (s + 1) * GLU_SUB)
        nxt = matmul(s + 1) if s + 1 < nsub else None
        gg = g_ref[:, cs].astype(F32)
        zz = z_ref[:, cs].astype(F32)
        den = (1.0 + jnp.exp2(-LOG2E * (acc + b_ref[:, cs]))) * (1.0 + jnp.exp2(-LOG2E * zz))
        o_ref[:, cs] = (gg * zz / den).astype(o_ref.dtype)
        acc = nxt


def _glu(gact, w, b_all, uz, j, tm=1024, tn=1024):
    t, e = gact.shape
    zoff = e // tn
    return pl.pallas_call(
        _glu_kernel,
        grid=(e // tn, t // tm),
        in_specs=[pl.BlockSpec((tm, e), lambda c, r: (r, 0)),
                  pl.BlockSpec((e, tn), lambda c, r: (0, c)),
                  pl.BlockSpec((None, 1, tn), lambda c, r: (j, 0, c)),
                  pl.BlockSpec((tm, tn), lambda c, r: (r, c)),
                  pl.BlockSpec((tm, tn), lambda c, r: (r, c + zoff))],
        out_specs=pl.BlockSpec((tm, tn), lambda c, r: (r, c)),
        out_shape=jax.ShapeDtypeStruct((t, e), BF16),
        compiler_params=_cparams(),
        name="glu",
    )(gact, w, b_all, gact, uz)


def _out_proj_kernel(y_ref, w_ref, x_ref, g_ref, *o_refs, last):
    xn = x_ref[...] + jnp.dot(y_ref[...], w_ref[...], preferred_element_type=F32)
    inv = lax.rsqrt(jnp.mean(xn * xn, axis=-1, keepdims=True) + NORM_EPS)
    hn = xn * inv * g_ref[...]
    if last:
        o_refs[0][...] = hn
    else:
        o_refs[0][...] = xn
        o_refs[1][...] = hn.astype(o_refs[1].dtype)


def _out_proj(y, w, x, gains, layer, last, tm=512):
    t, e = y.shape
    d = w.shape[1]
    row = pl.BlockSpec((tm, d), lambda i: (i, 0))
    if last:
        out_shape = jax.ShapeDtypeStruct((t, d), F32)
        out_specs = row
    else:
        out_shape = (jax.ShapeDtypeStruct((t, d), F32), jax.ShapeDtypeStruct((t, d), BF16))
        out_specs = (row, row)
    return pl.pallas_call(
        functools.partial(_out_proj_kernel, last=last),
        grid=(t // tm,),
        in_specs=[pl.BlockSpec((tm, e), lambda i: (i, 0)),
                  pl.BlockSpec((e, d), lambda i: (0, 0), pipeline_mode=pl.Buffered(1)),
                  row,
                  pl.BlockSpec((None, 1, d), lambda i: (layer, 0, 0))],
        out_specs=out_specs,
        out_shape=out_shape,
        compiler_params=_cparams(),
        name="out_proj",
    )(y, w, x, gains)


def kernel(x, norm_g, final_norm_g, pool_w_in, pool_w_grp, pool_scale, pool_w_out, ssm_w_in, ssm_a_re, ssm_a_im, ssm_log_dt, ssm_b_re, ssm_b_im, ssm_c_re, ssm_c_im, ssm_d, ssm_w_glu, ssm_b_glu, ssm_w_out):
    nb, seq_len, d = x.shape
    depth = norm_g.shape[0]
    segs = LANES // nb
    ni = seq_len // (CHUNK * segs)
    assert nb * segs == LANES and ni * segs * CHUNK == seq_len
    t = nb * seq_len

    gains = jnp.concatenate([norm_g, final_norm_g[None]], axis=0)[:, None, :]
    pool_scale3 = pool_scale[:, None, :]
    ssm_b_glu3 = ssm_b_glu[:, None, :]
    ssm_tables = _ssm_tables(ssm_a_re, ssm_a_im, ssm_log_dt, ssm_b_re, ssm_b_im,
                             ssm_c_re, ssm_c_im, ssm_d, ni * CHUNK)

    xs = x.reshape(nb, segs, ni, CHUNK, d).transpose(3, 2, 0, 1, 4).reshape(t, d)
    h = xs
    for layer in range(depth):
        j = layer // 2
        if layer % 2 == 0:
            uz, (w_out,) = _in_proj(h, gains, layer, pool_w_in, j, [pool_w_out])
            y = _pool_mixer(uz, pool_w_grp, pool_scale3, j, nb, ni)
        else:
            uz, (w_out, w_glu) = _in_proj(h, gains, layer, ssm_w_in, j, [ssm_w_out, ssm_w_glu])
            gact = _ssm_mixer(uz, ssm_tables, j, nb, ni)
            y = _glu(gact, w_glu, ssm_b_glu3, uz, j)
        last = layer == depth - 1
        res = _out_proj(y, w_out, xs, gains, layer + 1, last)
        if last:
            return res.reshape(CHUNK, ni, nb, segs, d).transpose(2, 3, 1, 0, 4).reshape(nb, seq_len, d)
        xs, h = res
```
